```python
import math
import jax, jax.numpy as jnp
from jax import lax
import numpy as np

D_MODEL = 1024
BATCH = 1
SEQ = 16384
DEPTH = 4

D_A = 512
HEAD_DIM_A = 128
N_HEADS_A = D_A // HEAD_DIM_A
CHUNK_A = 64
N_HEADS_B = 8
HEAD_DIM_B = 64
D_B = N_HEADS_B * HEAD_DIM_B
IDX_HEADS = 8
IDX_DIM = 64
TOPK_MAX = 256
Q_BLOCK = 128
N_BUCKETS = 32
MAX_DISTANCE = 128
ALPHA = (2 * DEPTH) ** 0.25
BETA = (8 * DEPTH) ** -0.25
LN_EPS = 1e-5
RMS_EPS = 1e-6

SPLITS = [D_A, D_A, D_A, D_A, D_B, D_B, D_B, D_B, IDX_HEADS * IDX_DIM, IDX_DIM, IDX_HEADS, D_MODEL, D_MODEL]
N_IN = sum(SPLITS)

kernel_name = "hgrn2_dsa_gated_hybrid_deepnorm"


def layer_norm(x, g, b):
    xf = x.astype(jnp.float32)
    mu = jnp.mean(xf, axis=-1, keepdims=True)
    var = jnp.mean(jnp.square(xf - mu), axis=-1, keepdims=True)
    return ((xf - mu) * lax.rsqrt(var + LN_EPS) * g.astype(jnp.float32) + b.astype(jnp.float32)).astype(x.dtype)


def t5_bucket(dist):
    max_exact = N_BUCKETS // 2
    d = jnp.maximum(dist, 0)
    df = jnp.maximum(d, 1).astype(jnp.float32)
    large = max_exact + (jnp.log(df / max_exact) / math.log(MAX_DISTANCE / max_exact)
                         * (N_BUCKETS - max_exact)).astype(jnp.int32)
    large = jnp.minimum(large, N_BUCKETS - 1)
    return jnp.where(d < max_exact, d, large)


def hgrn2_mix(q, f_logit, i, lb):
    B, L, _ = q.shape
    nc = L // CHUNK_A
    f32 = jnp.float32
    fl = f_logit.astype(f32)
    lbf = lb.astype(f32)
    log_f = jnp.logaddexp(jnp.log(lbf), jnp.log1p(-lbf) + jax.nn.log_sigmoid(fl))
    k = (1.0 - lbf) * jax.nn.sigmoid(-fl)

    def heads(t):
        return t.astype(f32).reshape(B, nc, CHUNK_A, N_HEADS_A, HEAD_DIM_A).transpose(1, 0, 3, 2, 4)

    qh, kh, vh, gh = heads(q), heads(k), heads(i), heads(log_f)
    bcum = jnp.cumsum(gh, axis=-2)
    causal = jnp.tril(jnp.ones((CHUNK_A, CHUNK_A), dtype=bool))[:, :, None]

    def step(S, inp):
        qc, kc, vc, bc = inp
        o_inter = jnp.einsum('bhtk,bhkv->bhtv', qc * jnp.exp(bc), S)
        diff = bc[:, :, :, None, :] - bc[:, :, None, :, :]
        decay = jnp.where(causal, jnp.exp(jnp.where(causal, diff, 0.0)), 0.0)
        scores = jnp.einsum('bhtk,bhtsk,bhsk->bhts', qc, decay, kc)
        o = o_inter + jnp.einsum('bhts,bhsv->bhtv', scores, vc)
        b_last = bc[:, :, -1:, :]
        S = jnp.exp(b_last[:, :, 0, :, None]) * S + jnp.einsum('bhsk,bhsv->bhkv', kc * jnp.exp(b_last - bc), vc)
        return S, o

    S0 = jnp.zeros((B, N_HEADS_A, HEAD_DIM_A, HEAD_DIM_A), f32)
    _, o = lax.scan(step, S0, (qh, kh, vh, bcum))
    return o.transpose(1, 0, 3, 2, 4).reshape(B, L, N_HEADS_A, HEAD_DIM_A)


def dsa_mix(q, k, v, q_idx, k_idx, w_idx, rel_bias):
    B, L, _ = q.shape
    f32 = jnp.float32
    topk = min(TOPK_MAX, L // 4)
    nb = L // Q_BLOCK
    kh = k.reshape(B, L, N_HEADS_B, HEAD_DIM_B)
    vh = v.reshape(B, L, N_HEADS_B, HEAD_DIM_B)
    kif = k_idx.astype(f32)
    key_pos = jnp.arange(L, dtype=jnp.int32)
    bidx = jnp.arange(B)[:, None, None]

    def to_blocks(t):
        return t.reshape(B, nb, Q_BLOCK, t.shape[-1]).transpose(1, 0, 2, 3)

    def block(args):
        qb, qib, wb, start = args
        qpos = start + jnp.arange(Q_BLOCK, dtype=jnp.int32)
        causal = key_pos[None, :] <= qpos[:, None]
        qi = qib.astype(f32).reshape(B, Q_BLOCK, IDX_HEADS, IDX_DIM)
        dots = jnp.einsum('bthd,bsd->bths', qi, kif) * (IDX_DIM ** -0.5)
        score = jnp.einsum('bth,bths->bts', wb.astype(f32) * (IDX_HEADS ** -0.5), jax.nn.relu(dots))
        score = jnp.where(causal[None], score, -jnp.inf)
        _, idx = lax.top_k(score, topk)
        valid = idx <= qpos[None, :, None]
        ks = kh[bidx, idx]
        vs = vh[bidx, idx]
        qh = qb.reshape(B, Q_BLOCK, N_HEADS_B, HEAD_DIM_B)
        logits = jnp.einsum('bthd,btkhd->bhtk', qh, ks).astype(f32) * (HEAD_DIM_B ** -0.5)
        bias = rel_bias[t5_bucket(qpos[None, :, None] - idx)].astype(f32)
        logits = logits + bias.transpose(0, 3, 1, 2)
        logits = jnp.where(valid[:, None], logits, -jnp.inf)
        p = jax.nn.softmax(logits, axis=-1)
        o = jnp.einsum('bhtk,btkhd->bthd', p.astype(vs.dtype), vs)
        return o.reshape(B, Q_BLOCK, D_B)

    starts = jnp.arange(nb, dtype=jnp.int32) * Q_BLOCK
    o = lax.map(block, (to_blocks(q), to_blocks(q_idx), to_blocks(w_idx), starts))
    return o.transpose(1, 0, 2, 3).reshape(B, L, D_B)


def hybrid_layer(x, w_in, b_in, w_up_a, w_up_b, w_out, lb, norm_a_g, rel_bias, ln_g, ln_b):
    B, L, _ = x.shape
    proj = jnp.einsum('bld,dn->bln', x, w_in) + b_in
    cuts = [int(c) for c in np.cumsum(SPLITS)[:-1]]
    qa, fa, ia, za, qb, kb, vb, zb, qi, ki, wi, ga, gb = jnp.split(proj, cuts, axis=-1)
    oa = hgrn2_mix(qa, fa, ia, lb)
    oa = oa * lax.rsqrt(jnp.mean(jnp.square(oa), axis=-1, keepdims=True) + RMS_EPS)
    oa = oa.reshape(B, L, D_A) * norm_a_g.astype(jnp.float32)
    ua = jnp.einsum('blc,cd->bld', (oa.astype(x.dtype) * jax.nn.silu(za)), w_up_a)
    ob = dsa_mix(qb, kb, vb, qi, ki, wi, rel_bias)
    ub = jnp.einsum('blc,cd->bld', ob * jax.nn.silu(zb), w_up_b)
    merged = jax.nn.sigmoid(ga) * ua + jax.nn.sigmoid(gb) * ub
    y = jnp.einsum('bld,de->ble', merged, w_out)
    return layer_norm(ALPHA * x + y, ln_g, ln_b)


def setup_inputs(seed: int = 0) -> dict:
    key = jax.random.key(seed)
    ks = jax.random.split(key, 12)
    f32 = jnp.float32
    x = jax.random.normal(ks[0], (BATCH, SEQ, D_MODEL), f32)
    w_in = jax.random.normal(ks[1], (DEPTH, D_MODEL, N_IN), f32) * D_MODEL ** -0.5
    b_in = jax.random.normal(ks[2], (DEPTH, N_IN), f32) * 0.02
    w_up_a = jax.random.normal(ks[3], (DEPTH, D_A, D_MODEL), f32) * (D_A ** -0.5) * BETA
    w_up_b = jax.random.normal(ks[4], (DEPTH, D_B, D_MODEL), f32) * (D_B ** -0.5) * BETA
    w_out = jax.random.normal(ks[5], (DEPTH, D_MODEL, D_MODEL), f32) * (D_MODEL ** -0.5) * BETA
    lb_logits = jax.random.normal(ks[6], (DEPTH, D_A), f32) * 0.5
    norm_a_g = 1.0 + 0.02 * jax.random.normal(ks[7], (DEPTH, D_A), f32)
    rel_bias = jax.random.normal(ks[8], (N_BUCKETS, N_HEADS_B), f32) * 0.5
    ln_g = 1.0 + 0.02 * jax.random.normal(ks[9], (DEPTH, D_MODEL), f32)
    ln_b = 0.02 * jax.random.normal(ks[10], (DEPTH, D_MODEL), f32)
    return {"x": x, "w_in": w_in, "b_in": b_in, "w_up_a": w_up_a, "w_up_b": w_up_b, "w_out": w_out,
            "lb_logits": lb_logits, "norm_a_g": norm_a_g, "rel_bias": rel_bias, "ln_g": ln_g, "ln_b": ln_b}


def reference(x, w_in, b_in, w_up_a, w_up_b, w_out, lb_logits, norm_a_g, rel_bias, ln_g, ln_b):
    lbs = jnp.cumsum(jax.nn.softmax(lb_logits.astype(jnp.float32), axis=0), axis=0)
    lbs = lbs - lbs[0:1]
    h = x
    for layer in range(DEPTH):
        h = hybrid_layer(h, w_in[layer], b_in[layer], w_up_a[layer], w_up_b[layer], w_out[layer],
                         lbs[layer], norm_a_g[layer], rel_bias, ln_g[layer], ln_b[layer])
    return h
```

```python
import functools
import math

import jax
import jax.numpy as jnp
from jax import lax
from jax.experimental import pallas as pl
from jax.experimental.pallas import tpu as pltpu

F32 = jnp.float32
BF16 = jnp.bfloat16
I32 = jnp.int32

D_MODEL = 1024
D_A = 512
HEAD_DIM_A = 128
N_HEADS_A = D_A // HEAD_DIM_A
N_HEADS_B = 8
HEAD_DIM_B = 64
D_B = N_HEADS_B * HEAD_DIM_B
IDX_HEADS = 8
IDX_DIM = 64
TOPK_MAX = 256
N_BUCKETS = 32
MAX_DISTANCE = 128
LN_EPS = 1e-5
RMS_EPS = 1e-6

LANES = 128
VMEM_LIMIT_BYTES = 58 * 1024 * 1024

HG_ROWS = 512
HG_CHUNK = 64
HG_SUB = 16

TQ = 128
KB = 128
TS = 512
NEAR = 256
MASK_NEG = -1e30

INT_MIN = -2147483648
NEG_INF_KEY = -2139095041


def _cparams(sem):
    return pltpu.CompilerParams(dimension_semantics=sem, vmem_limit_bytes=VMEM_LIMIT_BYTES)


def _proj_kernel(x_ref, w_ref, b_ref, o_ref):
    acc = jnp.dot(x_ref[...], w_ref[...], preferred_element_type=F32)
    o_ref[...] = (acc + b_ref[...]).astype(o_ref.dtype)


def _proj(x16, w16, b, out_dtype, tm, tn):
    m, k = x16.shape
    n = w16.shape[1]
    return pl.pallas_call(
        _proj_kernel,
        grid=(n // tn, m // tm),
        in_specs=[pl.BlockSpec((tm, k), lambda j, i: (i, 0)),
                  pl.BlockSpec((k, tn), lambda j, i: (0, j)),
                  pl.BlockSpec((1, tn), lambda j, i: (0, j))],
        out_specs=pl.BlockSpec((tm, tn), lambda j, i: (i, j)),
        out_shape=jax.ShapeDtypeStruct((m, n), out_dtype),
        compiler_params=_cparams(("arbitrary", "arbitrary")),
        name="proj",
    )(x16, w16, b)


def _proj_t_kernel(wt_ref, x_ref, b_ref, o_ref):
    acc = lax.dot_general(wt_ref[...], x_ref[...], (((1,), (1,)), ((), ())),
                          preferred_element_type=F32)
    o_ref[...] = (acc + b_ref[...]).astype(o_ref.dtype).reshape(o_ref.shape)


def _proj_t(wt16, x16, bcol, out_dtype, tm, blocked):
    m, k = x16.shape
    n = wt16.shape[0]
    if blocked:
        out_shape = jax.ShapeDtypeStruct((m // tm, n, tm), out_dtype)
        out_spec = pl.BlockSpec((1, n, tm), lambda i: (i, 0, 0))
    else:
        out_shape = jax.ShapeDtypeStruct((n, m), out_dtype)
        out_spec = pl.BlockSpec((n, tm), lambda i: (0, i))
    return pl.pallas_call(
        _proj_t_kernel,
        grid=(m // tm,),
        in_specs=[pl.BlockSpec((n, k), lambda i: (0, 0)),
                  pl.BlockSpec((tm, k), lambda i: (i, 0)),
                  pl.BlockSpec((n, 1), lambda i: (0, 0))],
        out_specs=out_spec,
        out_shape=out_shape,
        compiler_params=_cparams(("arbitrary",)),
        name="proj_t",
    )(wt16, x16, bcol)


def _split3_dot(t16, x):
    x1 = x.astype(BF16)
    r1 = x - x1.astype(F32)
    x2 = r1.astype(BF16)
    r2 = r1 - x2.astype(F32)
    x3 = r2.astype(BF16)
    return (jnp.dot(t16, x1, preferred_element_type=F32)
            + jnp.dot(t16, x2, preferred_element_type=F32)
            + jnp.dot(t16, x3, preferred_element_type=F32))


def _hgrn_kernel(lb_ref, g_ref, q_ref, f_ref, i_ref, z_ref, tril_ref, o_ref, st_ref):
    @pl.when(pl.program_id(1) == 0)
    def _():
        st_ref[...] = jnp.zeros_like(st_ref)

    rows = q_ref.shape[0]
    q = q_ref[...]
    fl = f_ref[...]
    v = i_ref[...]
    lb = lb_ref[...]
    log_lb = jnp.log(lb)
    log_1m = jnp.log(1.0 - lb)
    log_sig = jnp.minimum(fl, 0.0) - jnp.log(1.0 + jnp.exp(-jnp.abs(fl)))
    bb = log_1m + log_sig
    mx = jnp.maximum(log_lb, bb)
    log_f = mx + jnp.log(1.0 + jnp.exp(-jnp.abs(log_lb - bb)))
    kk = (1.0 - lb) * jax.nn.sigmoid(-fl)
    b = _split3_dot(tril_ref[...], log_f)

    row_l = lax.broadcasted_iota(I32, (rows, 1), 0) % HG_SUB
    ones16 = jnp.ones((LANES, LANES), BF16)
    qk0 = (q * kk).astype(BF16)
    o = jnp.dot(qk0, ones16, preferred_element_type=F32) * v
    for d in range(1, HG_SUB):
        valid = (row_l + d) < HG_SUB
        qd = pltpu.roll(q, rows - d, 0)
        bd = pltpu.roll(b, rows - d, 0)
        dec = jnp.exp(jnp.where(valid, bd - b, 0.0))
        e = jnp.where(valid, qd * kk * dec, 0.0).astype(BF16)
        c = jnp.dot(e, ones16, preferred_element_type=F32) * v
        o = o + pltpu.roll(c, d, 0)

    tl = lax.broadcasted_iota(I32, (HG_CHUNK, HG_CHUNK), 0)
    sl = lax.broadcasted_iota(I32, (HG_CHUNK, HG_CHUNK), 1)
    mask1 = ((tl // 32) == (sl // 32)) & ((tl % 32) >= 16) & ((sl % 32) < 16)
    rl = lax.broadcasted_iota(I32, (HG_CHUNK, 1), 0)
    hi32 = rl >= 32
    hi16 = (rl % 32) >= 16
    grp1 = rl >= 32

    st = st_ref[...]
    outs = []
    for c in range(rows // HG_CHUNK):
        sl_c = slice(c * HG_CHUNK, (c + 1) * HG_CHUNK)
        qc, kc, vc, bc = q[sl_c], kk[sl_c], v[sl_c], b[sl_c]
        b_last = bc[HG_CHUNK - 1:HG_CHUNK, :]
        r2 = bc[31:32, :]
        r1 = jnp.where(grp1, bc[47:48, :], bc[15:16, :])
        q2 = jnp.where(hi32, qc * jnp.exp(jnp.where(hi32, bc - r2, 0.0)), 0.0)
        k2 = jnp.where(hi32, 0.0, kc * jnp.exp(jnp.where(hi32, 0.0, r2 - bc)))
        q1 = jnp.where(hi16, qc * jnp.exp(jnp.where(hi16, bc - r1, 0.0)), 0.0)
        k1 = jnp.where(hi16, 0.0, kc * jnp.exp(jnp.where(hi16, 0.0, r1 - bc)))
        s2 = lax.dot_general(q2.astype(BF16), k2.astype(BF16), (((1,), (1,)), ((), ())),
                             preferred_element_type=F32)
        s1 = lax.dot_general(q1.astype(BF16), k1.astype(BF16), (((1,), (1,)), ((), ())),
                             preferred_element_type=F32)
        p = s2 + jnp.where(mask1, s1, 0.0)
        o_c = jnp.dot(p.astype(BF16), vc.astype(BF16), preferred_element_type=F32)
        qe = (qc * jnp.exp(bc)).astype(BF16)
        o_c = o_c + lax.dot_general(qe, st.astype(BF16), (((1,), (1,)), ((), ())),
                                    preferred_element_type=F32)
        ke = (kc * jnp.exp(b_last - bc)).astype(BF16)
        upd = lax.dot_general(vc.astype(BF16), ke, (((0,), (0,)), ((), ())),
                              preferred_element_type=F32)
        st = st * jnp.exp(b_last) + upd
        outs.append(o_c)
    st_ref[...] = st
    o = o + jnp.concatenate(outs, axis=0)

    ms = jnp.mean(o * o, axis=-1, keepdims=True)
    oa = o * lax.rsqrt(ms + RMS_EPS) * g_ref[...]
    z = z_ref[...]
    o_ref[...] = (oa * (z * jax.nn.sigmoid(z))).astype(o_ref.dtype)


def _hgrn(a_in, g_in, lb, norm_g, tril16):
    L = a_in.shape[0]
    nh = N_HEADS_A
    blk = lambda off: pl.BlockSpec((HG_ROWS, HEAD_DIM_A), lambda h, s: (s, off + h))
    return pl.pallas_call(
        _hgrn_kernel,
        grid=(nh, L // HG_ROWS),
        in_specs=[pl.BlockSpec((1, HEAD_DIM_A), lambda h, s: (0, h)),
                  pl.BlockSpec((1, HEAD_DIM_A), lambda h, s: (0, h)),
                  blk(0), blk(nh), blk(2 * nh),
                  pl.BlockSpec((HG_ROWS, HEAD_DIM_A), lambda h, s: (s, h)),
                  pl.BlockSpec((HG_ROWS, HG_ROWS), lambda h, s: (0, 0))],
        out_specs=pl.BlockSpec((HG_ROWS, HEAD_DIM_A), lambda h, s: (s, h)),
        out_shape=jax.ShapeDtypeStruct((L, D_A), BF16),
        scratch_shapes=[pltpu.VMEM((HEAD_DIM_A, HEAD_DIM_A), F32)],
        compiler_params=_cparams(("arbitrary", "arbitrary")),
        name="hgrn2",
    )(lb, norm_g, a_in, a_in, a_in, g_in, tril16)


def _bias_table_kernel(rb_ref, o_ref):
    n_c = o_ref.shape[1]
    c = lax.broadcasted_iota(I32, (n_c, TQ), 0)
    t = lax.broadcasted_iota(I32, (n_c, TQ), 1)
    dist = t + KB - c
    max_exact = N_BUCKETS // 2
    d = jnp.maximum(dist, 0)
    df = jnp.maximum(d, 1).astype(F32)
    large = max_exact + (jnp.log(df / max_exact) / math.log(MAX_DISTANCE / max_exact)
                         * (N_BUCKETS - max_exact)).astype(I32)
    large = jnp.minimum(large, N_BUCKETS - 1)
    bucket = jnp.where(d < max_exact, d, large)
    for h in range(N_HEADS_B):
        acc = jnp.zeros((n_c, TQ), F32)
        for k in range(N_BUCKETS):
            acc = jnp.where(bucket == k, rb_ref[k, h], acc)
        o_ref[h] = acc - rb_ref[N_BUCKETS - 1, h]


def _bias_table(rel_bias):
    return pl.pallas_call(
        _bias_table_kernel,
        in_specs=[pl.BlockSpec(memory_space=pltpu.SMEM)],
        out_specs=pl.BlockSpec(memory_space=pltpu.VMEM),
        out_shape=jax.ShapeDtypeStruct((N_HEADS_B, NEAR + KB, TQ), F32),
        name="t5_bias_table",
    )(rel_bias)


def _dsa_kernel(qbt_ref, qit_ref, wit_ref, zb_ref, kb_ref, vt_ref, ki_ref, bt_ref, o_ref,
                key_ref, acc_ref, m_ref, l_ref, *, topk, idx_bits):
    i = pl.program_id(0)
    t_idx = i * TQ + lax.broadcasted_iota(I32, (1, TQ), 1)
    n_kb = TS // KB
    nt = (i + n_kb) // n_kb
    row_kb = lax.broadcasted_iota(I32, (KB, 1), 0)
    row_ts = lax.broadcasted_iota(I32, (TS, 1), 0)

    qit = qit_ref[...]
    w = wit_ref[...] * (IDX_HEADS ** -0.5)
    qi_pairs = [jnp.concatenate([qit[2 * p * IDX_DIM:(2 * p + 1) * IDX_DIM, :],
                                 qit[(2 * p + 1) * IDX_DIM:(2 * p + 2) * IDX_DIM, :]], axis=1)
                for p in range(IDX_HEADS // 2)]

    def score_tile(j, carry):
        kt = ki_ref[pl.ds(j * n_kb, n_kb)].reshape(TS, IDX_DIM)
        acc = jnp.zeros((TS, TQ), F32)
        for p in range(IDX_HEADS // 2):
            dd = jnp.dot(kt, qi_pairs[p], preferred_element_type=F32)
            acc = acc + w[2 * p:2 * p + 1, :] * jnp.maximum(dd[:, :TQ], 0.0)
            acc = acc + w[2 * p + 1:2 * p + 2, :] * jnp.maximum(dd[:, TQ:], 0.0)
        s_idx = j * TS + row_ts
        sc = jnp.where(s_idx <= t_idx, acc, -jnp.inf)
        sc = sc + 0.0
        bits = pltpu.bitcast(sc, I32)
        keys = bits ^ ((bits >> 31) & 0x7FFFFFFF)
        key_ref[pl.ds(j * n_kb, n_kb)] = keys.reshape(n_kb, KB, TQ)
        return carry

    lax.fori_loop(0, nt, score_tile, 0)

    def count_ge(thr_signed):
        def body(j, cnt):
            keys = key_ref[pl.ds(j * n_kb, n_kb)].reshape(TS // 8, 8, TQ)
            return cnt + jnp.sum((keys >= thr_signed).astype(I32), axis=0)
        cnt = lax.fori_loop(0, nt, body, jnp.zeros((8, TQ), I32))
        return jnp.sum(cnt, axis=0, keepdims=True)

    def bit_step(bi, tb):
        bit = jnp.left_shift(jnp.int32(1), 31 - bi)
        cand = tb | bit
        cnt = count_ge(cand ^ INT_MIN)
        return jnp.where(cnt >= topk, cand, tb)

    tb = lax.fori_loop(0, 32, bit_step, jnp.zeros((1, TQ), I32))
    thr = tb ^ INT_MIN
    thr = jnp.maximum(thr, NEG_INF_KEY + 1)
    cnt_ge_thr = count_ge(thr)
    excess = (cnt_ge_thr > topk)

    @pl.when(jnp.max(excess.astype(I32)) > 0)
    def _():
        cnt_gt = count_ge(thr + 1)
        need = topk - cnt_gt

        def count_eq_below(cut):
            def body(j, cnt):
                keys = key_ref[pl.ds(j * n_kb, n_kb)].reshape(TS, TQ)
                s_idx = j * TS + row_ts
                hit = (keys == thr) & (s_idx < cut)
                return cnt + jnp.sum(hit.astype(I32).reshape(TS // 8, 8, TQ), axis=0)
            cnt = lax.fori_loop(0, nt, body, jnp.zeros((8, TQ), I32))
            return jnp.sum(cnt, axis=0, keepdims=True)

        def idx_step(bi, cut):
            cand = cut | jnp.left_shift(jnp.int32(1), idx_bits - 1 - bi)
            return jnp.where(count_eq_below(cand) < need, cand, cut)

        cut = lax.fori_loop(0, idx_bits, idx_step, jnp.zeros((1, TQ), I32))
        def demote(j, carry):
            keys = key_ref[pl.ds(j * n_kb, n_kb)].reshape(TS, TQ)
            s_idx = j * TS + row_ts
            drop = excess & (keys == thr) & (s_idx > cut)
            key_ref[pl.ds(j * n_kb, n_kb)] = jnp.where(drop, keys - 1, keys).reshape(n_kb, KB, TQ)
            return carry
        lax.fori_loop(0, nt, demote, 0)

    qbt = qbt_ref[...]
    zq = jnp.zeros((HEAD_DIM_B, TQ), BF16)
    qb_pairs = [jnp.concatenate(
        [jnp.concatenate([qbt[2 * p * HEAD_DIM_B:(2 * p + 1) * HEAD_DIM_B, :], zq], axis=1),
         jnp.concatenate([zq, qbt[(2 * p + 1) * HEAD_DIM_B:(2 * p + 2) * HEAD_DIM_B, :]], axis=1)],
        axis=0) for p in range(N_HEADS_B // 2)]

    m_ref[...] = jnp.full(m_ref.shape, MASK_NEG, F32)
    l_ref[...] = jnp.zeros(l_ref.shape, F32)
    acc_ref[...] = jnp.zeros(acc_ref.shape, F32)

    def attend(kt, vt, madd, bias_of_head):
        for p in range(N_HEADS_B // 2):
            lg = jnp.dot(kt[:, 2 * p * HEAD_DIM_B:(2 * p + 2) * HEAD_DIM_B], qb_pairs[p],
                         preferred_element_type=F32)
            for hh in range(2):
                h = 2 * p + hh
                lo = lg[:, hh * TQ:(hh + 1) * TQ] + madd
                if bias_of_head is not None:
                    lo = lo + bias_of_head(h)
                m_old = m_ref[h:h + 1, :]
                m_new = jnp.maximum(m_old, jnp.max(lo, axis=0, keepdims=True))
                alpha = jnp.exp(m_old - m_new)
                pe = jnp.exp(lo - m_new)
                l_ref[h:h + 1, :] = alpha * l_ref[h:h + 1, :] + jnp.sum(pe, axis=0, keepdims=True)
                m_ref[h:h + 1, :] = m_new
                rs = slice(h * HEAD_DIM_B, (h + 1) * HEAD_DIM_B)
                pv = jnp.dot(vt[rs, :], pe.astype(BF16), preferred_element_type=F32)
                acc_ref[rs, :] = alpha * acc_ref[rs, :] + pv

    far_end = (i - 1) * KB
    nf = (i + n_kb - 2) // n_kb

    def far_tile(j, carry):
        kt = kb_ref[pl.ds(j * n_kb, n_kb)].reshape(TS, D_B)
        vt = jnp.concatenate([vt_ref[j * n_kb + c] for c in range(n_kb)], axis=1)
        keys = key_ref[pl.ds(j * n_kb, n_kb)].reshape(TS, TQ)
        s_idx = j * TS + row_ts
        madd = jnp.where((keys >= thr) & (s_idx < far_end), 0.0, MASK_NEG)
        attend(kt, vt, madd, None)
        return carry

    lax.fori_loop(0, nf, far_tile, 0)

    nb0 = jnp.maximum(i - 1, 0)
    toff = jnp.where(i == 0, KB, 0)
    kt = kb_ref[pl.ds(nb0, NEAR // KB)].reshape(NEAR, D_B)
    vt = jnp.concatenate([vt_ref[nb0 + c] for c in range(NEAR // KB)], axis=1)
    keys = key_ref[pl.ds(nb0, NEAR // KB)].reshape(NEAR, TQ)
    madd = jnp.where(keys >= thr, 0.0, MASK_NEG)
    attend(kt, vt, madd, lambda h: bt_ref[h, pl.ds(pl.multiple_of(toff, KB), NEAR), :])

    inv = 1.0 / l_ref[...]
    ot = jnp.concatenate([acc_ref[h * HEAD_DIM_B:(h + 1) * HEAD_DIM_B, :] * inv[h:h + 1, :]
                          for h in range(N_HEADS_B)], axis=0)
    ob = ot.T
    z = zb_ref[...]
    o_ref[...] = (ob * (z * jax.nn.sigmoid(z))).astype(o_ref.dtype)


def _dsa(qbt, qit, wit, g_in, kb3, vt3, ki3, btab, topk):
    L = qbt.shape[1]
    n_blk = L // KB
    idx_bits = max(1, (L - 1).bit_length())
    whole = pl.BlockSpec(memory_space=pltpu.VMEM)
    return pl.pallas_call(
        functools.partial(_dsa_kernel, topk=topk, idx_bits=idx_bits),
        grid=(L // TQ,),
        in_specs=[pl.BlockSpec((D_B, TQ), lambda i: (0, i)),
                  pl.BlockSpec((IDX_HEADS * IDX_DIM, TQ), lambda i: (0, i)),
                  pl.BlockSpec((16, TQ), lambda i: (0, i)),
                  pl.BlockSpec((TQ, D_B), lambda i: (i, 1)),
                  whole, whole, whole, whole],
        out_specs=pl.BlockSpec((TQ, D_B), lambda i: (i, 0)),
        out_shape=jax.ShapeDtypeStruct((L, D_B), BF16),
        scratch_shapes=[pltpu.VMEM((n_blk + TS // KB, KB, TQ), I32),
                        pltpu.VMEM((D_B, TQ), F32),
                        pltpu.VMEM((N_HEADS_B, TQ), F32),
                        pltpu.VMEM((N_HEADS_B, TQ), F32)],
        compiler_params=_cparams(("arbitrary",)),
        name="dsa",
    )(qbt, qit, wit, g_in, kb3, vt3, ki3, btab)


def _out_kernel(x_ref, ga_ref, gb_ref, a_ref, bq_ref, wa_ref, wb_ref, wo_ref, lg_ref, lb_ref,
                o_ref, o16_ref, *, alpha):
    ua = jnp.dot(a_ref[...], wa_ref[...], preferred_element_type=F32)
    ub = jnp.dot(bq_ref[...], wb_ref[...], preferred_element_type=F32)
    merged = jax.nn.sigmoid(ga_ref[...]) * ua + jax.nn.sigmoid(gb_ref[...]) * ub
    y = jnp.dot(merged.astype(BF16), wo_ref[...], preferred_element_type=F32)
    r = alpha * x_ref[...] + y
    mu = jnp.mean(r, axis=-1, keepdims=True)
    var = jnp.mean(jnp.square(r - mu), axis=-1, keepdims=True)
    out = (r - mu) * lax.rsqrt(var + LN_EPS) * lg_ref[...] + lb_ref[...]
    o_ref[...] = out
    o16_ref[...] = out.astype(BF16)


def _out_block(x, g_in, ga_in, gb_in, wa16, wb16, wo16, ln_g, ln_b, alpha, tm):
    L = x.shape[0]
    const = lambda shape: pl.BlockSpec(shape, lambda i: (0, 0))
    return pl.pallas_call(
        functools.partial(_out_kernel, alpha=alpha),
        grid=(L // tm,),
        in_specs=[pl.BlockSpec((tm, D_MODEL), lambda i: (i, 0)),
                  pl.BlockSpec((tm, D_MODEL), lambda i: (i, 1)),
                  pl.BlockSpec((tm, D_MODEL), lambda i: (i, 2)),
                  pl.BlockSpec((tm, D_A), lambda i: (i, 0)),
                  pl.BlockSpec((tm, D_B), lambda i: (i, 0)),
                  const((D_A, D_MODEL)), const((D_B, D_MODEL)), const((D_MODEL, D_MODEL)),
                  const((1, D_MODEL)), const((1, D_MODEL))],
        out_specs=[pl.BlockSpec((tm, D_MODEL), lambda i: (i, 0)),
                   pl.BlockSpec((tm, D_MODEL), lambda i: (i, 0))],
        out_shape=[jax.ShapeDtypeStruct((L, D_MODEL), F32),
                   jax.ShapeDtypeStruct((L, D_MODEL), BF16)],
        compiler_params=_cparams(("arbitrary",)),
        name="out_block",
    )(x, g_in, g_in, ga_in, gb_in, wa16, wb16, wo16, ln_g, ln_b)


def _layer(h, h16, w_in, b_in, w_up_a, w_up_b, w_out, lb, norm_g, btab, ln_g, ln_b, tril16,
           alpha, topk):
    L = h.shape[0]
    tm = min(512, L)
    c0 = 4 * D_A
    col = lambda a, n: (w_in[:, a:a + n], b_in[a:a + n])

    def normal(a, n, dtype, tn, scale=1.0):
        w, b = col(a, n)
        return _proj(h16, (w * scale).astype(BF16), (b * scale)[None, :], dtype, tm, tn)

    def transposed(a, n, dtype, blocked, scale=1.0, pad_to=None):
        w, b = col(a, n)
        wt = (w * scale).T
        bc = (b * scale)[:, None]
        if pad_to is not None:
            wt = jnp.pad(wt, ((0, pad_to - n), (0, 0)))
            bc = jnp.pad(bc, ((0, pad_to - n), (0, 0)))
        return _proj_t(wt.astype(BF16), h16, bc, dtype, KB if blocked else tm, blocked)

    a_in = normal(0, 3 * D_A, F32, 512)
    gparts = [col(3 * D_A, D_A), col(c0 + 3 * D_B, D_B), col(N_IN - 2 * D_MODEL, 2 * D_MODEL)]
    g_in = _proj(h16, jnp.concatenate([w for w, _ in gparts], axis=1).astype(BF16),
                 jnp.concatenate([b for _, b in gparts])[None, :], F32, tm, 512)
    qk_scale = HEAD_DIM_B ** -0.5
    qbt = transposed(c0, D_B, BF16, False, scale=qk_scale)
    kb3 = normal(c0 + D_B, D_B, BF16, 512).reshape(L // KB, KB, D_B)
    vt3 = transposed(c0 + 2 * D_B, D_B, BF16, True)
    ci = c0 + 4 * D_B
    qit = transposed(ci, IDX_HEADS * IDX_DIM, BF16, False, scale=IDX_DIM ** -0.5)
    ki3 = normal(ci + IDX_HEADS * IDX_DIM, IDX_DIM, BF16, IDX_DIM).reshape(L // KB, KB, IDX_DIM)
    wit = transposed(ci + IDX_HEADS * IDX_DIM + IDX_DIM, IDX_HEADS, F32, False, pad_to=16)

    ga_in = _hgrn(a_in, g_in, lb[None, :], norm_g[None, :], tril16)
    gb_in = _dsa(qbt, qit, wit, g_in, kb3, vt3, ki3, btab, topk)
    return _out_block(h, g_in, ga_in, gb_in, w_up_a.astype(BF16), w_up_b.astype(BF16),
                      w_out.astype(BF16), ln_g[None, :], ln_b[None, :], alpha, tm)


N_IN = 4 * D_A + 4 * D_B + IDX_HEADS * IDX_DIM + IDX_DIM + IDX_HEADS + 2 * D_MODEL


def kernel(x, w_in, b_in, w_up_a, w_up_b, w_out, lb_logits, norm_a_g, rel_bias, ln_g, ln_b):
    depth = w_in.shape[0]
    batch, L, _ = x.shape
    alpha = (2 * depth) ** 0.25
    topk = min(TOPK_MAX, L // 4)
    lbs = jnp.cumsum(jax.nn.softmax(lb_logits.astype(F32), axis=0), axis=0)
    lbs = lbs - lbs[0:1]
    r = jnp.arange(HG_ROWS)
    tril16 = ((r[:, None] >= r[None, :]) & ((r[:, None] // HG_CHUNK) == (r[None, :] // HG_CHUNK))).astype(BF16)
    btab = _bias_table(rel_bias.astype(F32))
    outs = []
    for bi in range(batch):
        h = x[bi]
        h16 = h.astype(BF16)
        for layer in range(depth):
            h, h16 = _layer(h, h16, w_in[layer], b_in[layer], w_up_a[layer], w_up_b[layer], w_out[layer],
                            lbs[layer], norm_a_g[layer], btab, ln_g[layer], ln_b[layer], tril16, alpha, topk)
        outs.append(h)
    return jnp.stack(outs, axis=0)
```

```python
import functools
import math

import jax
import jax.numpy as jnp
from jax import lax
from jax.experimental import pallas as pl
from jax.experimental.pallas import tpu as pltpu

F32 = jnp.float32
BF16 = jnp.bfloat16
I32 = jnp.int32

D_MODEL = 1024
D_A = 512
HEAD_DIM_A = 128
N_HEADS_A = D_A // HEAD_DIM_A
N_HEADS_B = 8
HEAD_DIM_B = 64
D_B = N_HEADS_B * HEAD_DIM_B
IDX_HEADS = 8
IDX_DIM = 64
TOPK_MAX = 256
N_BUCKETS = 32
MAX_DISTANCE = 128
LN_EPS = 1e-5
RMS_EPS = 1e-6

LANES = 128
VMEM_LIMIT_BYTES = 58 * 1024 * 1024

HG_ROWS = 512
HG_CHUNK = 64
HG_SUB = 16

TQ = 128
KB = 128
TS = 512
NEAR = 256
MASK_NEG = -1e30
LOG2E = 1.4426950408889634

INT_MIN = -2147483648
NEG_INF_KEY = -2139095041


def _cparams(sem):
    return pltpu.CompilerParams(dimension_semantics=sem, vmem_limit_bytes=VMEM_LIMIT_BYTES)


def _proj_kernel(x_ref, w_ref, b_ref, o_ref):
    acc = jnp.dot(x_ref[...], w_ref[...], preferred_element_type=F32)
    o_ref[...] = (acc + b_ref[...]).astype(o_ref.dtype)


def _proj(x16, w16, b, out_dtype, tm, tn):
    m, k = x16.shape
    n = w16.shape[1]
    return pl.pallas_call(
        _proj_kernel,
        grid=(n // tn, m // tm),
        in_specs=[pl.BlockSpec((tm, k), lambda j, i: (i, 0)),
                  pl.BlockSpec((k, tn), lambda j, i: (0, j)),
                  pl.BlockSpec((1, tn), lambda j, i: (0, j))],
        out_specs=pl.BlockSpec((tm, tn), lambda j, i: (i, j)),
        out_shape=jax.ShapeDtypeStruct((m, n), out_dtype),
        compiler_params=_cparams(("arbitrary", "arbitrary")),
        name="proj",
    )(x16, w16, b)


def _proj_t_kernel(wt_ref, x_ref, b_ref, o_ref):
    acc = lax.dot_general(wt_ref[...], x_ref[...], (((1,), (1,)), ((), ())),
                          preferred_element_type=F32)
    o_ref[...] = (acc + b_ref[...]).astype(o_ref.dtype).reshape(o_ref.shape)


def _proj_t(wt16, x16, bcol, out_dtype, tm, blocked):
    m, k = x16.shape
    n = wt16.shape[0]
    if blocked:
        out_shape = jax.ShapeDtypeStruct((m // tm, n, tm), out_dtype)
        out_spec = pl.BlockSpec((1, n, tm), lambda i: (i, 0, 0))
    else:
        out_shape = jax.ShapeDtypeStruct((n, m), out_dtype)
        out_spec = pl.BlockSpec((n, tm), lambda i: (0, i))
    return pl.pallas_call(
        _proj_t_kernel,
        grid=(m // tm,),
        in_specs=[pl.BlockSpec((n, k), lambda i: (0, 0)),
                  pl.BlockSpec((tm, k), lambda i: (i, 0)),
                  pl.BlockSpec((n, 1), lambda i: (0, 0))],
        out_specs=out_spec,
        out_shape=out_shape,
        compiler_params=_cparams(("arbitrary",)),
        name="proj_t",
    )(wt16, x16, bcol)


def _split3_dot(t16, x):
    x1 = x.astype(BF16)
    r1 = x - x1.astype(F32)
    x2 = r1.astype(BF16)
    r2 = r1 - x2.astype(F32)
    x3 = r2.astype(BF16)
    return (jnp.dot(t16, x1, preferred_element_type=F32)
            + jnp.dot(t16, x2, preferred_element_type=F32)
            + jnp.dot(t16, x3, preferred_element_type=F32))


def _hgrn_kernel(lb_ref, g_ref, q_ref, f_ref, i_ref, z_ref, tril_ref, o_ref, st_ref):
    @pl.when(pl.program_id(1) == 0)
    def _():
        st_ref[...] = jnp.zeros_like(st_ref)

    rows = q_ref.shape[0]
    q = q_ref[...]
    fl = f_ref[...]
    v = i_ref[...]
    lb = lb_ref[...]
    log_lb = jnp.log(lb)
    log_1m = jnp.log(1.0 - lb)
    log_sig = jnp.minimum(fl, 0.0) - jnp.log(1.0 + jnp.exp(-jnp.abs(fl)))
    bb = log_1m + log_sig
    mx = jnp.maximum(log_lb, bb)
    log_f = mx + jnp.log(1.0 + jnp.exp(-jnp.abs(log_lb - bb)))
    kk = (1.0 - lb) * jax.nn.sigmoid(-fl)
    b = _split3_dot(tril_ref[...], log_f)

    row_l = lax.broadcasted_iota(I32, (rows, 1), 0) % HG_SUB
    ones16 = jnp.ones((LANES, LANES), BF16)
    qk0 = (q * kk).astype(BF16)
    o = jnp.dot(qk0, ones16, preferred_element_type=F32) * v
    for d in range(1, HG_SUB):
        valid = (row_l + d) < HG_SUB
        qd = pltpu.roll(q, rows - d, 0)
        bd = pltpu.roll(b, rows - d, 0)
        dec = jnp.exp(jnp.where(valid, bd - b, 0.0))
        e = jnp.where(valid, qd * kk * dec, 0.0).astype(BF16)
        c = jnp.dot(e, ones16, preferred_element_type=F32) * v
        o = o + pltpu.roll(c, d, 0)

    tl = lax.broadcasted_iota(I32, (HG_CHUNK, HG_CHUNK), 0)
    sl = lax.broadcasted_iota(I32, (HG_CHUNK, HG_CHUNK), 1)
    mask1 = ((tl // 32) == (sl // 32)) & ((tl % 32) >= 16) & ((sl % 32) < 16)
    rl = lax.broadcasted_iota(I32, (HG_CHUNK, 1), 0)
    hi32 = rl >= 32
    hi16 = (rl % 32) >= 16
    grp1 = rl >= 32

    st = st_ref[...]
    outs = []
    for c in range(rows // HG_CHUNK):
        sl_c = slice(c * HG_CHUNK, (c + 1) * HG_CHUNK)
        qc, kc, vc, bc = q[sl_c], kk[sl_c], v[sl_c], b[sl_c]
        b_last = bc[HG_CHUNK - 1:HG_CHUNK, :]
        r2 = bc[31:32, :]
        r1 = jnp.where(grp1, bc[47:48, :], bc[15:16, :])
        q2 = jnp.where(hi32, qc * jnp.exp(jnp.where(hi32, bc - r2, 0.0)), 0.0)
        k2 = jnp.where(hi32, 0.0, kc * jnp.exp(jnp.where(hi32, 0.0, r2 - bc)))
        q1 = jnp.where(hi16, qc * jnp.exp(jnp.where(hi16, bc - r1, 0.0)), 0.0)
        k1 = jnp.where(hi16, 0.0, kc * jnp.exp(jnp.where(hi16, 0.0, r1 - bc)))
        s2 = lax.dot_general(q2.astype(BF16), k2.astype(BF16), (((1,), (1,)), ((), ())),
                             preferred_element_type=F32)
        s1 = lax.dot_general(q1.astype(BF16), k1.astype(BF16), (((1,), (1,)), ((), ())),
                             preferred_element_type=F32)
        p = s2 + jnp.where(mask1, s1, 0.0)
        o_c = jnp.dot(p.astype(BF16), vc.astype(BF16), preferred_element_type=F32)
        qe = (qc * jnp.exp(bc)).astype(BF16)
        o_c = o_c + lax.dot_general(qe, st.astype(BF16), (((1,), (1,)), ((), ())),
                                    preferred_element_type=F32)
        ke = (kc * jnp.exp(b_last - bc)).astype(BF16)
        upd = lax.dot_general(vc.astype(BF16), ke, (((0,), (0,)), ((), ())),
                              preferred_element_type=F32)
        st = st * jnp.exp(b_last) + upd
        outs.append(o_c)
    st_ref[...] = st
    o = o + jnp.concatenate(outs, axis=0)

    ms = jnp.mean(o * o, axis=-1, keepdims=True)
    oa = o * lax.rsqrt(ms + RMS_EPS) * g_ref[...]
    z = z_ref[...]
    o_ref[...] = (oa * (z * jax.nn.sigmoid(z))).astype(o_ref.dtype)


def _hgrn(a_in, g_in, lb, norm_g, tril16):
    L = a_in.shape[0]
    nh = N_HEADS_A
    blk = lambda off: pl.BlockSpec((HG_ROWS, HEAD_DIM_A), lambda h, s: (s, off + h))
    return pl.pallas_call(
        _hgrn_kernel,
        grid=(nh, L // HG_ROWS),
        in_specs=[pl.BlockSpec((1, HEAD_DIM_A), lambda h, s: (0, h)),
                  pl.BlockSpec((1, HEAD_DIM_A), lambda h, s: (0, h)),
                  blk(0), blk(nh), blk(2 * nh),
                  pl.BlockSpec((HG_ROWS, HEAD_DIM_A), lambda h, s: (s, h)),
                  pl.BlockSpec((HG_ROWS, HG_ROWS), lambda h, s: (0, 0))],
        out_specs=pl.BlockSpec((HG_ROWS, HEAD_DIM_A), lambda h, s: (s, h)),
        out_shape=jax.ShapeDtypeStruct((L, D_A), BF16),
        scratch_shapes=[pltpu.VMEM((HEAD_DIM_A, HEAD_DIM_A), F32)],
        compiler_params=_cparams(("arbitrary", "arbitrary")),
        name="hgrn2",
    )(lb, norm_g, a_in, a_in, a_in, g_in, tril16)


def _bias_table_kernel(rb_ref, o_ref):
    n_c = o_ref.shape[1]
    c = lax.broadcasted_iota(I32, (n_c, TQ), 0)
    t = lax.broadcasted_iota(I32, (n_c, TQ), 1)
    dist = t + KB - c
    max_exact = N_BUCKETS // 2
    d = jnp.maximum(dist, 0)
    df = jnp.maximum(d, 1).astype(F32)
    large = max_exact + (jnp.log(df / max_exact) / math.log(MAX_DISTANCE / max_exact)
                         * (N_BUCKETS - max_exact)).astype(I32)
    large = jnp.minimum(large, N_BUCKETS - 1)
    bucket = jnp.where(d < max_exact, d, large)
    for h in range(N_HEADS_B):
        acc = jnp.zeros((n_c, TQ), F32)
        for k in range(N_BUCKETS):
            acc = jnp.where(bucket == k, rb_ref[k, h], acc)
        o_ref[h] = (acc - rb_ref[N_BUCKETS - 1, h]) * LOG2E


def _bias_table(rel_bias):
    return pl.pallas_call(
        _bias_table_kernel,
        in_specs=[pl.BlockSpec(memory_space=pltpu.SMEM)],
        out_specs=pl.BlockSpec(memory_space=pltpu.VMEM),
        out_shape=jax.ShapeDtypeStruct((N_HEADS_B, NEAR + KB, TQ), F32),
        name="t5_bias_table",
    )(rel_bias)


def _dsa_kernel(qbt_ref, qit_ref, wit_ref, zb_ref, kb_ref, vt_ref, ki_ref, bt_ref, o_ref,
                key_ref, lg_ref, mt_ref, acc_ref, m_ref, l_ref, *, topk, idx_bits):
    i = pl.program_id(0)
    t_idx = i * TQ + lax.broadcasted_iota(I32, (1, TQ), 1)
    n_kb = TS // KB
    nt = (i + n_kb) // n_kb
    row_kb = lax.broadcasted_iota(I32, (KB, 1), 0)
    row_ts = lax.broadcasted_iota(I32, (TS, 1), 0)

    qit = qit_ref[...]
    w = wit_ref[...] * (IDX_HEADS ** -0.5)
    qi_pairs = [jnp.concatenate([qit[2 * p * IDX_DIM:(2 * p + 1) * IDX_DIM, :],
                                 qit[(2 * p + 1) * IDX_DIM:(2 * p + 2) * IDX_DIM, :]], axis=1)
                for p in range(IDX_HEADS // 2)]

    def score_tile(j, carry):
        kt = ki_ref[pl.ds(j * n_kb, n_kb)].reshape(TS, IDX_DIM)
        acc = jnp.zeros((TS, TQ), F32)
        for p in range(IDX_HEADS // 2):
            dd = jnp.dot(kt, qi_pairs[p], preferred_element_type=F32)
            acc = acc + w[2 * p:2 * p + 1, :] * jnp.maximum(dd[:, :TQ], 0.0)
            acc = acc + w[2 * p + 1:2 * p + 2, :] * jnp.maximum(dd[:, TQ:], 0.0)
        s_idx = j * TS + row_ts
        sc = jnp.where(s_idx <= t_idx, acc, -jnp.inf)
        sc = sc + 0.0
        bits = pltpu.bitcast(sc, I32)
        keys = bits ^ ((bits >> 31) & 0x7FFFFFFF)
        key_ref[pl.ds(j * n_kb, n_kb)] = keys.reshape(n_kb, KB, TQ)
        return carry

    lax.fori_loop(0, nt, score_tile, 0)

    def count_ge(thr_signed):
        def body(j, cnt):
            keys = key_ref[pl.ds(j * n_kb, n_kb)].reshape(TS // 8, 8, TQ)
            return cnt + jnp.sum((keys >= thr_signed).astype(I32), axis=0)
        cnt = lax.fori_loop(0, nt, body, jnp.zeros((8, TQ), I32))
        return jnp.sum(cnt, axis=0, keepdims=True)

    def bit_step(bi, tb):
        bit = jnp.left_shift(jnp.int32(1), 31 - bi)
        cand = tb | bit
        cnt = count_ge(cand ^ INT_MIN)
        return jnp.where(cnt >= topk, cand, tb)

    tb = lax.fori_loop(0, 32, bit_step, jnp.zeros((1, TQ), I32))
    thr = tb ^ INT_MIN
    thr = jnp.maximum(thr, NEG_INF_KEY + 1)
    cnt_ge_thr = count_ge(thr)
    excess = (cnt_ge_thr > topk)

    @pl.when(jnp.max(excess.astype(I32)) > 0)
    def _():
        cnt_gt = count_ge(thr + 1)
        need = topk - cnt_gt

        def count_eq_below(cut):
            def body(j, cnt):
                keys = key_ref[pl.ds(j * n_kb, n_kb)].reshape(TS, TQ)
                s_idx = j * TS + row_ts
                hit = (keys == thr) & (s_idx < cut)
                return cnt + jnp.sum(hit.astype(I32).reshape(TS // 8, 8, TQ), axis=0)
            cnt = lax.fori_loop(0, nt, body, jnp.zeros((8, TQ), I32))
            return jnp.sum(cnt, axis=0, keepdims=True)

        def idx_step(bi, cut):
            cand = cut | jnp.left_shift(jnp.int32(1), idx_bits - 1 - bi)
            return jnp.where(count_eq_below(cand) < need, cand, cut)

        cut = lax.fori_loop(0, idx_bits, idx_step, jnp.zeros((1, TQ), I32))
        def demote(j, carry):
            keys = key_ref[pl.ds(j * n_kb, n_kb)].reshape(TS, TQ)
            s_idx = j * TS + row_ts
            drop = excess & (keys == thr) & (s_idx > cut)
            key_ref[pl.ds(j * n_kb, n_kb)] = jnp.where(drop, keys - 1, keys).reshape(n_kb, KB, TQ)
            return carry
        lax.fori_loop(0, nt, demote, 0)

    qbt = qbt_ref[...]
    zq = jnp.zeros((HEAD_DIM_B, TQ), BF16)
    qb_pairs = [jnp.concatenate(
        [jnp.concatenate([qbt[2 * p * HEAD_DIM_B:(2 * p + 1) * HEAD_DIM_B, :], zq], axis=1),
         jnp.concatenate([zq, qbt[(2 * p + 1) * HEAD_DIM_B:(2 * p + 2) * HEAD_DIM_B, :]], axis=1)],
        axis=0) for p in range(N_HEADS_B // 2)]

    m_ref[...] = jnp.full(m_ref.shape, MASK_NEG, F32)
    l_ref[...] = jnp.zeros(l_ref.shape, F32)
    acc_ref[...] = jnp.zeros(acc_ref.shape, F32)

    def stage1(slot, kt, madd, bias_of_head):
        n = kt.shape[0]
        for p in range(N_HEADS_B // 2):
            lg = jnp.dot(kt[:, 2 * p * HEAD_DIM_B:(2 * p + 2) * HEAD_DIM_B], qb_pairs[p],
                         preferred_element_type=F32)
            for hh in range(2):
                h = 2 * p + hh
                lo = lg[:, hh * TQ:(hh + 1) * TQ] + madd
                if bias_of_head is not None:
                    lo = lo + bias_of_head(h)
                lg_ref[slot, h, 0:n, :] = lo
                mt_ref[slot, h] = jnp.broadcast_to(jnp.max(lo, axis=0, keepdims=True), (8, TQ))

    def stage2(slot, vt):
        n = vt.shape[1]
        for h in range(N_HEADS_B):
            m_old = m_ref[h]
            m_new = jnp.maximum(m_old, mt_ref[slot, h])
            alpha = jnp.exp2(m_old - m_new)
            m_ref[h] = m_new
            pe = jnp.exp2(lg_ref[slot, h, 0:n, :] - m_new[0:1, :])
            l_ref[h] = alpha * l_ref[h] + jnp.sum(pe, axis=0, keepdims=True)
            rs = slice(h * HEAD_DIM_B, (h + 1) * HEAD_DIM_B)
            pv = jnp.dot(vt[rs, :], pe.astype(BF16), preferred_element_type=F32)
            acc_ref[rs, :] = alpha[0:1, :] * acc_ref[rs, :] + pv

    far_end = (i - 1) * KB
    nf = (i + n_kb - 2) // n_kb

    def far_stage1(j):
        kt = kb_ref[pl.ds(j * n_kb, n_kb)].reshape(TS, D_B)
        keys = key_ref[pl.ds(j * n_kb, n_kb)].reshape(TS, TQ)
        s_idx = j * TS + row_ts
        madd = jnp.where((keys >= thr) & (s_idx < far_end), 0.0, MASK_NEG)
        stage1(j % 2, kt, madd, None)

    def far_stage2(j):
        vt = jnp.concatenate([vt_ref[j * n_kb + c] for c in range(n_kb)], axis=1)
        stage2(j % 2, vt)

    nb0 = jnp.maximum(i - 1, 0)

    def near_stage1(slot):
        toff = jnp.where(i == 0, KB, 0)
        kt = kb_ref[pl.ds(nb0, NEAR // KB)].reshape(NEAR, D_B)
        keys = key_ref[pl.ds(nb0, NEAR // KB)].reshape(NEAR, TQ)
        madd = jnp.where(keys >= thr, 0.0, MASK_NEG)
        stage1(slot, kt, madd, lambda h: bt_ref[h, pl.ds(pl.multiple_of(toff, KB), NEAR), :])

    def far_tile(j, carry):
        far_stage1(j)
        far_stage2(j)
        return carry
    lax.fori_loop(0, nf, far_tile, 0)
    near_stage1(0)
    stage2(0, jnp.concatenate([vt_ref[nb0 + c] for c in range(NEAR // KB)], axis=1))

    ot = jnp.concatenate([acc_ref[h * HEAD_DIM_B:(h + 1) * HEAD_DIM_B, :] * (1.0 / l_ref[h, 0:1, :])
                          for h in range(N_HEADS_B)], axis=0)
    ob = ot.T
    z = zb_ref[...]
    o_ref[...] = (ob * (z * jax.nn.sigmoid(z))).astype(o_ref.dtype)


def _dsa(qbt, qit, wit, g_in, kb3, vt3, ki3, btab, topk):
    L = qbt.shape[1]
    n_blk = L // KB
    idx_bits = max(1, (L - 1).bit_length())
    whole = pl.BlockSpec(memory_space=pltpu.VMEM)
    return pl.pallas_call(
        functools.partial(_dsa_kernel, topk=topk, idx_bits=idx_bits),
        grid=(L // TQ,),
        in_specs=[pl.BlockSpec((D_B, TQ), lambda i: (0, i)),
                  pl.BlockSpec((IDX_HEADS * IDX_DIM, TQ), lambda i: (0, i)),
                  pl.BlockSpec((16, TQ), lambda i: (0, i)),
                  pl.BlockSpec((TQ, D_B), lambda i: (i, 1)),
                  whole, whole, whole, whole],
        out_specs=pl.BlockSpec((TQ, D_B), lambda i: (i, 0)),
        out_shape=jax.ShapeDtypeStruct((L, D_B), BF16),
        scratch_shapes=[pltpu.VMEM((n_blk, KB, TQ), I32),
                        pltpu.VMEM((2, N_HEADS_B, TS, TQ), F32),
                        pltpu.VMEM((2, N_HEADS_B, 8, TQ), F32),
                        pltpu.VMEM((D_B, TQ), F32),
                        pltpu.VMEM((N_HEADS_B, 8, TQ), F32),
                        pltpu.VMEM((N_HEADS_B, 8, TQ), F32)],
        compiler_params=_cparams(("arbitrary",)),
        name="dsa",
    )(qbt, qit, wit, g_in, kb3, vt3, ki3, btab)


def _out_kernel(x_ref, ga_ref, gb_ref, a_ref, bq_ref, wa_ref, wb_ref, wo_ref, lg_ref, lb_ref,
                o_ref, o16_ref, *, alpha):
    ua = jnp.dot(a_ref[...], wa_ref[...], preferred_element_type=F32)
    ub = jnp.dot(bq_ref[...], wb_ref[...], preferred_element_type=F32)
    merged = jax.nn.sigmoid(ga_ref[...]) * ua + jax.nn.sigmoid(gb_ref[...]) * ub
    y = jnp.dot(merged.astype(BF16), wo_ref[...], preferred_element_type=F32)
    r = alpha * x_ref[...] + y
    mu = jnp.mean(r, axis=-1, keepdims=True)
    var = jnp.mean(jnp.square(r - mu), axis=-1, keepdims=True)
    out = (r - mu) * lax.rsqrt(var + LN_EPS) * lg_ref[...] + lb_ref[...]
    o_ref[...] = out
    o16_ref[...] = out.astype(BF16)


def _out_block(x, g_in, ga_in, gb_in, wa16, wb16, wo16, ln_g, ln_b, alpha, tm):
    L = x.shape[0]
    const = lambda shape: pl.BlockSpec(shape, lambda i: (0, 0))
    return pl.pallas_call(
        functools.partial(_out_kernel, alpha=alpha),
        grid=(L // tm,),
        in_specs=[pl.BlockSpec((tm, D_MODEL), lambda i: (i, 0)),
                  pl.BlockSpec((tm, D_MODEL), lambda i: (i, 1)),
                  pl.BlockSpec((tm, D_MODEL), lambda i: (i, 2)),
                  pl.BlockSpec((tm, D_A), lambda i: (i, 0)),
                  pl.BlockSpec((tm, D_B), lambda i: (i, 0)),
                  const((D_A, D_MODEL)), const((D_B, D_MODEL)), const((D_MODEL, D_MODEL)),
                  const((1, D_MODEL)), const((1, D_MODEL))],
        out_specs=[pl.BlockSpec((tm, D_MODEL), lambda i: (i, 0)),
                   pl.BlockSpec((tm, D_MODEL), lambda i: (i, 0))],
        out_shape=[jax.ShapeDtypeStruct((L, D_MODEL), F32),
                   jax.ShapeDtypeStruct((L, D_MODEL), BF16)],
        compiler_params=_cparams(("arbitrary",)),
        name="out_block",
    )(x, g_in, g_in, ga_in, gb_in, wa16, wb16, wo16, ln_g, ln_b)


def _layer(h, h16, w_in, b_in, w_up_a, w_up_b, w_out, lb, norm_g, btab, ln_g, ln_b, tril16,
           alpha, topk):
    L = h.shape[0]
    tm = min(512, L)
    c0 = 4 * D_A
    col = lambda a, n: (w_in[:, a:a + n], b_in[a:a + n])

    def normal(a, n, dtype, tn, scale=1.0):
        w, b = col(a, n)
        return _proj(h16, (w * scale).astype(BF16), (b * scale)[None, :], dtype, tm, tn)

    def transposed(a, n, dtype, blocked, scale=1.0, pad_to=None):
        w, b = col(a, n)
        wt = (w * scale).T
        bc = (b * scale)[:, None]
        if pad_to is not None:
            wt = jnp.pad(wt, ((0, pad_to - n), (0, 0)))
            bc = jnp.pad(bc, ((0, pad_to - n), (0, 0)))
        return _proj_t(wt.astype(BF16), h16, bc, dtype, KB if blocked else tm, blocked)

    a_in = normal(0, 3 * D_A, F32, 512)
    gparts = [col(3 * D_A, D_A), col(c0 + 3 * D_B, D_B), col(N_IN - 2 * D_MODEL, 2 * D_MODEL)]
    g_in = _proj(h16, jnp.concatenate([w for w, _ in gparts], axis=1).astype(BF16),
                 jnp.concatenate([b for _, b in gparts])[None, :], F32, tm, 512)
    qbt = transposed(c0, D_B, BF16, False, scale=HEAD_DIM_B ** -0.5 * LOG2E)
    kb3 = normal(c0 + D_B, D_B, BF16, 512).reshape(L // KB, KB, D_B)
    vt3 = transposed(c0 + 2 * D_B, D_B, BF16, True)
    ci = c0 + 4 * D_B
    qit = transposed(ci, IDX_HEADS * IDX_DIM, BF16, False, scale=IDX_DIM ** -0.5)
    ki3 = normal(ci + IDX_HEADS * IDX_DIM, IDX_DIM, BF16, IDX_DIM).reshape(L // KB, KB, IDX_DIM)
    wit = transposed(ci + IDX_HEADS * IDX_DIM + IDX_DIM, IDX_HEADS, F32, False, pad_to=16)

    ga_in = _hgrn(a_in, g_in, lb[None, :], norm_g[None, :], tril16)
    gb_in = _dsa(qbt, qit, wit, g_in, kb3, vt3, ki3, btab, topk)
    return _out_block(h, g_in, ga_in, gb_in, w_up_a.astype(BF16), w_up_b.astype(BF16),
                      w_out.astype(BF16), ln_g[None, :], ln_b[None, :], alpha, tm)


N_IN = 4 * D_A + 4 * D_B + IDX_HEADS * IDX_DIM + IDX_DIM + IDX_HEADS + 2 * D_MODEL


def kernel(x, w_in, b_in, w_up_a, w_up_b, w_out, lb_logits, norm_a_g, rel_bias, ln_g, ln_b):
    depth = w_in.shape[0]
    batch, L, _ = x.shape
    alpha = (2 * depth) ** 0.25
    topk = min(TOPK_MAX, L // 4)
    lbs = jnp.cumsum(jax.nn.softmax(lb_logits.astype(F32), axis=0), axis=0)
    lbs = lbs - lbs[0:1]
    r = jnp.arange(HG_ROWS)
    tril16 = ((r[:, None] >= r[None, :]) & ((r[:, None] // HG_CHUNK) == (r[None, :] // HG_CHUNK))).astype(BF16)
    btab = _bias_table(rel_bias.astype(F32))
    outs = []
    for bi in range(batch):
        h = x[bi]
        h16 = h.astype(BF16)
        for layer in range(depth):
            h, h16 = _layer(h, h16, w_in[layer], b_in[layer], w_up_a[layer], w_up_b[layer], w_out[layer],
                            lbs[layer], norm_a_g[layer], btab, ln_g[layer], ln_b[layer], tril16, alpha, topk)
        outs.append(h)
    return jnp.stack(outs, axis=0)
```

```python
import functools
import math

import jax
import jax.numpy as jnp
from jax import lax
from jax.experimental import pallas as pl
from jax.experimental.pallas import tpu as pltpu

F32 = jnp.float32
BF16 = jnp.bfloat16
I32 = jnp.int32

D_MODEL = 1024
D_A = 512
HEAD_DIM_A = 128
N_HEADS_A = D_A // HEAD_DIM_A
N_HEADS_B = 8
HEAD_DIM_B = 64
D_B = N_HEADS_B * HEAD_DIM_B
IDX_HEADS = 8
IDX_DIM = 64
TOPK_MAX = 256
N_BUCKETS = 32
MAX_DISTANCE = 128
LN_EPS = 1e-5
RMS_EPS = 1e-6

LANES = 128
VMEM_LIMIT_BYTES = 58 * 1024 * 1024

HG_ROWS = 512
HG_CHUNK = 64
HG_SUB = 16

TQ = 128
KB = 128
TS = 512
NEAR = 256
MASK_NEG = -1e30
CAND_G = 16
CAND_M = 12
CAND_MIN_TILES = 9
LOG2E = 1.4426950408889634

INT_MIN = -2147483648
NEG_INF_KEY = -2139095041


def _cparams(sem):
    return pltpu.CompilerParams(dimension_semantics=sem, vmem_limit_bytes=VMEM_LIMIT_BYTES)


def _proj_kernel(x_ref, w_ref, b_ref, o_ref):
    acc = jnp.dot(x_ref[...], w_ref[...], preferred_element_type=F32)
    o_ref[...] = (acc + b_ref[...]).astype(o_ref.dtype)


def _proj(x16, w16, b, out_dtype, tm, tn):
    m, k = x16.shape
    n = w16.shape[1]
    return pl.pallas_call(
        _proj_kernel,
        grid=(n // tn, m // tm),
        in_specs=[pl.BlockSpec((tm, k), lambda j, i: (i, 0)),
                  pl.BlockSpec((k, tn), lambda j, i: (0, j)),
                  pl.BlockSpec((1, tn), lambda j, i: (0, j))],
        out_specs=pl.BlockSpec((tm, tn), lambda j, i: (i, j)),
        out_shape=jax.ShapeDtypeStruct((m, n), out_dtype),
        compiler_params=_cparams(("arbitrary", "arbitrary")),
        name="proj",
    )(x16, w16, b)


def _proj_t_kernel(wt_ref, x_ref, b_ref, o_ref):
    acc = lax.dot_general(wt_ref[...], x_ref[...], (((1,), (1,)), ((), ())),
                          preferred_element_type=F32)
    o_ref[...] = (acc + b_ref[...]).astype(o_ref.dtype).reshape(o_ref.shape)


def _proj_t(wt16, x16, bcol, out_dtype, tm, blocked):
    m, k = x16.shape
    n = wt16.shape[0]
    if blocked:
        out_shape = jax.ShapeDtypeStruct((m // tm, n, tm), out_dtype)
        out_spec = pl.BlockSpec((1, n, tm), lambda i: (i, 0, 0))
    else:
        out_shape = jax.ShapeDtypeStruct((n, m), out_dtype)
        out_spec = pl.BlockSpec((n, tm), lambda i: (0, i))
    return pl.pallas_call(
        _proj_t_kernel,
        grid=(m // tm,),
        in_specs=[pl.BlockSpec((n, k), lambda i: (0, 0)),
                  pl.BlockSpec((tm, k), lambda i: (i, 0)),
                  pl.BlockSpec((n, 1), lambda i: (0, 0))],
        out_specs=out_spec,
        out_shape=out_shape,
        compiler_params=_cparams(("arbitrary",)),
        name="proj_t",
    )(wt16, x16, bcol)


def _split3_dot(t16, x):
    x1 = x.astype(BF16)
    r1 = x - x1.astype(F32)
    x2 = r1.astype(BF16)
    r2 = r1 - x2.astype(F32)
    x3 = r2.astype(BF16)
    return (jnp.dot(t16, x1, preferred_element_type=F32)
            + jnp.dot(t16, x2, preferred_element_type=F32)
            + jnp.dot(t16, x3, preferred_element_type=F32))


def _hgrn_kernel(lb_ref, g_ref, q_ref, f_ref, i_ref, z_ref, tril_ref, o_ref, st_ref):
    @pl.when(pl.program_id(1) == 0)
    def _():
        st_ref[...] = jnp.zeros_like(st_ref)

    rows = q_ref.shape[0]
    q = q_ref[...]
    fl = f_ref[...]
    v = i_ref[...]
    lb = lb_ref[...]
    log_lb = jnp.log(lb)
    log_1m = jnp.log(1.0 - lb)
    log_sig = jnp.minimum(fl, 0.0) - jnp.log(1.0 + jnp.exp(-jnp.abs(fl)))
    bb = log_1m + log_sig
    mx = jnp.maximum(log_lb, bb)
    log_f = mx + jnp.log(1.0 + jnp.exp(-jnp.abs(log_lb - bb)))
    kk = (1.0 - lb) * jax.nn.sigmoid(-fl)
    b = _split3_dot(tril_ref[...], log_f)

    row_l = lax.broadcasted_iota(I32, (rows, 1), 0) % HG_SUB
    ones16 = jnp.ones((LANES, LANES), BF16)
    qk0 = (q * kk).astype(BF16)
    o = jnp.dot(qk0, ones16, preferred_element_type=F32) * v
    for d in range(1, HG_SUB):
        valid = (row_l + d) < HG_SUB
        qd = pltpu.roll(q, rows - d, 0)
        bd = pltpu.roll(b, rows - d, 0)
        dec = jnp.exp(jnp.where(valid, bd - b, 0.0))
        e = jnp.where(valid, qd * kk * dec, 0.0).astype(BF16)
        c = jnp.dot(e, ones16, preferred_element_type=F32) * v
        o = o + pltpu.roll(c, d, 0)

    tl = lax.broadcasted_iota(I32, (HG_CHUNK, HG_CHUNK), 0)
    sl = lax.broadcasted_iota(I32, (HG_CHUNK, HG_CHUNK), 1)
    mask1 = ((tl // 32) == (sl // 32)) & ((tl % 32) >= 16) & ((sl % 32) < 16)
    rl = lax.broadcasted_iota(I32, (HG_CHUNK, 1), 0)
    hi32 = rl >= 32
    hi16 = (rl % 32) >= 16
    grp1 = rl >= 32

    st = st_ref[...]
    outs = []
    for c in range(rows // HG_CHUNK):
        sl_c = slice(c * HG_CHUNK, (c + 1) * HG_CHUNK)
        qc, kc, vc, bc = q[sl_c], kk[sl_c], v[sl_c], b[sl_c]
        b_last = bc[HG_CHUNK - 1:HG_CHUNK, :]
        r2 = bc[31:32, :]
        r1 = jnp.where(grp1, bc[47:48, :], bc[15:16, :])
        q2 = jnp.where(hi32, qc * jnp.exp(jnp.where(hi32, bc - r2, 0.0)), 0.0)
        k2 = jnp.where(hi32, 0.0, kc * jnp.exp(jnp.where(hi32, 0.0, r2 - bc)))
        q1 = jnp.where(hi16, qc * jnp.exp(jnp.where(hi16, bc - r1, 0.0)), 0.0)
        k1 = jnp.where(hi16, 0.0, kc * jnp.exp(jnp.where(hi16, 0.0, r1 - bc)))
        s2 = lax.dot_general(q2.astype(BF16), k2.astype(BF16), (((1,), (1,)), ((), ())),
                             preferred_element_type=F32)
        s1 = lax.dot_general(q1.astype(BF16), k1.astype(BF16), (((1,), (1,)), ((), ())),
                             preferred_element_type=F32)
        p = s2 + jnp.where(mask1, s1, 0.0)
        o_c = jnp.dot(p.astype(BF16), vc.astype(BF16), preferred_element_type=F32)
        qe = (qc * jnp.exp(bc)).astype(BF16)
        o_c = o_c + lax.dot_general(qe, st.astype(BF16), (((1,), (1,)), ((), ())),
                                    preferred_element_type=F32)
        ke = (kc * jnp.exp(b_last - bc)).astype(BF16)
        upd = lax.dot_general(vc.astype(BF16), ke, (((0,), (0,)), ((), ())),
                              preferred_element_type=F32)
        st = st * jnp.exp(b_last) + upd
        outs.append(o_c)
    st_ref[...] = st
    o = o + jnp.concatenate(outs, axis=0)

    ms = jnp.mean(o * o, axis=-1, keepdims=True)
    oa = o * lax.rsqrt(ms + RMS_EPS) * g_ref[...]
    z = z_ref[...]
    o_ref[...] = (oa * (z * jax.nn.sigmoid(z))).astype(o_ref.dtype)


def _hgrn(a_in, g_in, lb, norm_g, tril16):
    L = a_in.shape[0]
    nh = N_HEADS_A
    blk = lambda off: pl.BlockSpec((HG_ROWS, HEAD_DIM_A), lambda h, s: (s, off + h))
    return pl.pallas_call(
        _hgrn_kernel,
        grid=(nh, L // HG_ROWS),
        in_specs=[pl.BlockSpec((1, HEAD_DIM_A), lambda h, s: (0, h)),
                  pl.BlockSpec((1, HEAD_DIM_A), lambda h, s: (0, h)),
                  blk(0), blk(nh), blk(2 * nh),
                  pl.BlockSpec((HG_ROWS, HEAD_DIM_A), lambda h, s: (s, h)),
                  pl.BlockSpec((HG_ROWS, HG_ROWS), lambda h, s: (0, 0))],
        out_specs=pl.BlockSpec((HG_ROWS, HEAD_DIM_A), lambda h, s: (s, h)),
        out_shape=jax.ShapeDtypeStruct((L, D_A), BF16),
        scratch_shapes=[pltpu.VMEM((HEAD_DIM_A, HEAD_DIM_A), F32)],
        compiler_params=_cparams(("arbitrary", "arbitrary")),
        name="hgrn2",
    )(lb, norm_g, a_in, a_in, a_in, g_in, tril16)


def _bias_table_kernel(rb_ref, o_ref):
    n_c = o_ref.shape[1]
    c = lax.broadcasted_iota(I32, (n_c, TQ), 0)
    t = lax.broadcasted_iota(I32, (n_c, TQ), 1)
    dist = t + KB - c
    max_exact = N_BUCKETS // 2
    d = jnp.maximum(dist, 0)
    df = jnp.maximum(d, 1).astype(F32)
    large = max_exact + (jnp.log(df / max_exact) / math.log(MAX_DISTANCE / max_exact)
                         * (N_BUCKETS - max_exact)).astype(I32)
    large = jnp.minimum(large, N_BUCKETS - 1)
    bucket = jnp.where(d < max_exact, d, large)
    for h in range(N_HEADS_B):
        acc = jnp.zeros((n_c, TQ), F32)
        for k in range(N_BUCKETS):
            acc = jnp.where(bucket == k, rb_ref[k, h], acc)
        o_ref[h] = (acc - rb_ref[N_BUCKETS - 1, h]) * LOG2E


def _bias_table(rel_bias):
    return pl.pallas_call(
        _bias_table_kernel,
        in_specs=[pl.BlockSpec(memory_space=pltpu.SMEM)],
        out_specs=pl.BlockSpec(memory_space=pltpu.VMEM),
        out_shape=jax.ShapeDtypeStruct((N_HEADS_B, NEAR + KB, TQ), F32),
        name="t5_bias_table",
    )(rel_bias)


def _dsa_kernel(qbt_ref, qit_ref, wit_ref, zb_ref, kb_ref, vt_ref, ki_ref, bt_ref, o_ref,
                key_ref, cand_ref, candk_ref, stat_ref, lg_ref, mt_ref, acc_ref, m_ref, l_ref,
                *, topk, idx_bits):
    i = pl.program_id(0)
    t_idx = i * TQ + lax.broadcasted_iota(I32, (1, TQ), 1)
    n_kb = TS // KB
    nt = (i + n_kb) // n_kb
    row_kb = lax.broadcasted_iota(I32, (KB, 1), 0)
    row_ts = lax.broadcasted_iota(I32, (TS, 1), 0)

    qit = qit_ref[...]
    w = wit_ref[...] * (IDX_HEADS ** -0.5)
    qi_pairs = [jnp.concatenate([qit[2 * p * IDX_DIM:(2 * p + 1) * IDX_DIM, :],
                                 qit[(2 * p + 1) * IDX_DIM:(2 * p + 2) * IDX_DIM, :]], axis=1)
                for p in range(IDX_HEADS // 2)]

    def score_tile(j, carry):
        kt = ki_ref[pl.ds(j * n_kb, n_kb)].reshape(TS, IDX_DIM)
        acc = jnp.zeros((TS, TQ), F32)
        for p in range(IDX_HEADS // 2):
            dd = jnp.dot(kt, qi_pairs[p], preferred_element_type=F32)
            acc = acc + w[2 * p:2 * p + 1, :] * jnp.maximum(dd[:, :TQ], 0.0)
            acc = acc + w[2 * p + 1:2 * p + 2, :] * jnp.maximum(dd[:, TQ:], 0.0)
        s_idx = j * TS + row_ts
        sc = jnp.where(s_idx <= t_idx, acc, -jnp.inf)
        sc = sc + 0.0
        bits = pltpu.bitcast(sc, I32)
        keys = bits ^ ((bits >> 31) & 0x7FFFFFFF)
        key_ref[pl.ds(j * n_kb, n_kb)] = keys.reshape(n_kb, KB, TQ)
        sc3 = sc.reshape(TS // 8, 8, TQ)
        for g in range(CAND_G):
            lst = [cand_ref[g, k] for k in range(CAND_M)]
            for c in range(TS // 8 // CAND_G):
                x = sc3[g + CAND_G * c]
                for k in range(CAND_M):
                    hi = jnp.maximum(lst[k], x)
                    x = jnp.minimum(lst[k], x)
                    lst[k] = hi
            for k in range(CAND_M):
                cand_ref[g, k] = lst[k]
        return carry

    cand_ref[...] = jnp.full(cand_ref.shape, -jnp.inf, F32)
    lax.fori_loop(0, nt, score_tile, 0)

    def to_keys(sc):
        bits = pltpu.bitcast(sc, I32)
        return bits ^ ((bits >> 31) & 0x7FFFFFFF)

    def count_ge(thr_signed):
        def body(j, cnt):
            keys = key_ref[pl.ds(j * n_kb, n_kb)].reshape(TS // 8, 8, TQ)
            return cnt + jnp.sum((keys >= thr_signed).astype(I32), axis=0)
        cnt = lax.fori_loop(0, nt, body, jnp.zeros((8, TQ), I32))
        return jnp.sum(cnt, axis=0, keepdims=True)

    def bisect(count_fn):
        def bit_step(bi, tb):
            cand = tb | jnp.left_shift(jnp.int32(1), 31 - bi)
            return jnp.where(count_fn(cand ^ INT_MIN) >= topk, cand, tb)
        tb = lax.fori_loop(0, 32, bit_step, jnp.zeros((1, TQ), I32))
        return jnp.maximum(tb ^ INT_MIN, NEG_INF_KEY + 1)

    stat_ref[2] = jnp.ones((8, TQ), I32)

    @pl.when(nt > CAND_MIN_TILES)
    def _():
        candk_ref[...] = to_keys(cand_ref[...].reshape(CAND_G * CAND_M, 8, TQ))

        def count_cand_ge(thr_signed):
            hit = (candk_ref[...] >= thr_signed).astype(I32)
            return jnp.sum(jnp.sum(hit, axis=0), axis=0, keepdims=True)

        thr_c = bisect(count_cand_ge)
        stat_ref[0] = jnp.broadcast_to(thr_c, (8, TQ))
        stat_ref[1] = jnp.broadcast_to(count_ge(thr_c), (8, TQ))
        stat_ref[2] = jnp.broadcast_to((count_ge(thr_c + 1) >= topk).astype(I32), (8, TQ))

    @pl.when(jnp.max(stat_ref[2]) > 0)
    def _():
        thr_f = bisect(count_ge)
        stat_ref[0] = jnp.broadcast_to(thr_f, (8, TQ))
        stat_ref[1] = jnp.broadcast_to(count_ge(thr_f), (8, TQ))

    thr = stat_ref[0, 0:1, :]
    cnt_ge_thr = stat_ref[1, 0:1, :]
    excess = (cnt_ge_thr > topk)

    @pl.when(jnp.max(excess.astype(I32)) > 0)
    def _():
        cnt_gt = count_ge(thr + 1)
        need = topk - cnt_gt

        def count_eq_below(cut):
            def body(j, cnt):
                keys = key_ref[pl.ds(j * n_kb, n_kb)].reshape(TS, TQ)
                s_idx = j * TS + row_ts
                hit = (keys == thr) & (s_idx < cut)
                return cnt + jnp.sum(hit.astype(I32).reshape(TS // 8, 8, TQ), axis=0)
            cnt = lax.fori_loop(0, nt, body, jnp.zeros((8, TQ), I32))
            return jnp.sum(cnt, axis=0, keepdims=True)

        def idx_step(bi, cut):
            cand = cut | jnp.left_shift(jnp.int32(1), idx_bits - 1 - bi)
            return jnp.where(count_eq_below(cand) < need, cand, cut)

        cut = lax.fori_loop(0, idx_bits, idx_step, jnp.zeros((1, TQ), I32))
        def demote(j, carry):
            keys = key_ref[pl.ds(j * n_kb, n_kb)].reshape(TS, TQ)
            s_idx = j * TS + row_ts
            drop = excess & (keys == thr) & (s_idx > cut)
            key_ref[pl.ds(j * n_kb, n_kb)] = jnp.where(drop, keys - 1, keys).reshape(n_kb, KB, TQ)
            return carry
        lax.fori_loop(0, nt, demote, 0)

    qbt = qbt_ref[...]
    zq = jnp.zeros((HEAD_DIM_B, TQ), BF16)
    qb_pairs = [jnp.concatenate(
        [jnp.concatenate([qbt[2 * p * HEAD_DIM_B:(2 * p + 1) * HEAD_DIM_B, :], zq], axis=1),
         jnp.concatenate([zq, qbt[(2 * p + 1) * HEAD_DIM_B:(2 * p + 2) * HEAD_DIM_B, :]], axis=1)],
        axis=0) for p in range(N_HEADS_B // 2)]

    m_ref[...] = jnp.full(m_ref.shape, MASK_NEG, F32)
    l_ref[...] = jnp.zeros(l_ref.shape, F32)
    acc_ref[...] = jnp.zeros(acc_ref.shape, F32)

    def stage1(slot, kt, madd, bias_of_head):
        n = kt.shape[0]
        for p in range(N_HEADS_B // 2):
            lg = jnp.dot(kt[:, 2 * p * HEAD_DIM_B:(2 * p + 2) * HEAD_DIM_B], qb_pairs[p],
                         preferred_element_type=F32)
            for hh in range(2):
                h = 2 * p + hh
                lo = lg[:, hh * TQ:(hh + 1) * TQ] + madd
                if bias_of_head is not None:
                    lo = lo + bias_of_head(h)
                lg_ref[slot, h, 0:n, :] = lo
                mt_ref[slot, h] = jnp.broadcast_to(jnp.max(lo, axis=0, keepdims=True), (8, TQ))

    def stage2(slot, vt):
        n = vt.shape[1]
        for h in range(N_HEADS_B):
            m_old = m_ref[h]
            m_new = jnp.maximum(m_old, mt_ref[slot, h])
            alpha = jnp.exp2(m_old - m_new)
            m_ref[h] = m_new
            pe = jnp.exp2(lg_ref[slot, h, 0:n, :] - m_new[0:1, :])
            l_ref[h] = alpha * l_ref[h] + jnp.sum(pe, axis=0, keepdims=True)
            rs = slice(h * HEAD_DIM_B, (h + 1) * HEAD_DIM_B)
            pv = jnp.dot(vt[rs, :], pe.astype(BF16), preferred_element_type=F32)
            acc_ref[rs, :] = alpha[0:1, :] * acc_ref[rs, :] + pv

    far_end = (i - 1) * KB
    nf = (i + n_kb - 2) // n_kb

    def far_stage1(j):
        kt = kb_ref[pl.ds(j * n_kb, n_kb)].reshape(TS, D_B)
        keys = key_ref[pl.ds(j * n_kb, n_kb)].reshape(TS, TQ)
        s_idx = j * TS + row_ts
        madd = jnp.where((keys >= thr) & (s_idx < far_end), 0.0, MASK_NEG)
        stage1(j % 2, kt, madd, None)

    def far_stage2(j):
        vt = jnp.concatenate([vt_ref[j * n_kb + c] for c in range(n_kb)], axis=1)
        stage2(j % 2, vt)

    nb0 = jnp.maximum(i - 1, 0)

    def near_stage1(slot):
        toff = jnp.where(i == 0, KB, 0)
        kt = kb_ref[pl.ds(nb0, NEAR // KB)].reshape(NEAR, D_B)
        keys = key_ref[pl.ds(nb0, NEAR // KB)].reshape(NEAR, TQ)
        madd = jnp.where(keys >= thr, 0.0, MASK_NEG)
        stage1(slot, kt, madd, lambda h: bt_ref[h, pl.ds(pl.multiple_of(toff, KB), NEAR), :])

    def far_tile(j, carry):
        far_stage1(j)
        far_stage2(j)
        return carry
    lax.fori_loop(0, nf, far_tile, 0)
    near_stage1(0)
    stage2(0, jnp.concatenate([vt_ref[nb0 + c] for c in range(NEAR // KB)], axis=1))

    ot = jnp.concatenate([acc_ref[h * HEAD_DIM_B:(h + 1) * HEAD_DIM_B, :] * (1.0 / l_ref[h, 0:1, :])
                          for h in range(N_HEADS_B)], axis=0)
    ob = ot.T
    z = zb_ref[...]
    o_ref[...] = (ob * (z * jax.nn.sigmoid(z))).astype(o_ref.dtype)


def _dsa(qbt, qit, wit, g_in, kb3, vt3, ki3, btab, topk):
    L = qbt.shape[1]
    n_blk = L // KB
    idx_bits = max(1, (L - 1).bit_length())
    whole = pl.BlockSpec(memory_space=pltpu.VMEM)
    return pl.pallas_call(
        functools.partial(_dsa_kernel, topk=topk, idx_bits=idx_bits),
        grid=(L // TQ,),
        in_specs=[pl.BlockSpec((D_B, TQ), lambda i: (0, i)),
                  pl.BlockSpec((IDX_HEADS * IDX_DIM, TQ), lambda i: (0, i)),
                  pl.BlockSpec((16, TQ), lambda i: (0, i)),
                  pl.BlockSpec((TQ, D_B), lambda i: (i, 1)),
                  whole, whole, whole, whole],
        out_specs=pl.BlockSpec((TQ, D_B), lambda i: (i, 0)),
        out_shape=jax.ShapeDtypeStruct((L, D_B), BF16),
        scratch_shapes=[pltpu.VMEM((n_blk, KB, TQ), I32),
                        pltpu.VMEM((CAND_G, CAND_M, 8, TQ), F32),
                        pltpu.VMEM((CAND_G * CAND_M, 8, TQ), I32),
                        pltpu.VMEM((3, 8, TQ), I32),
                        pltpu.VMEM((2, N_HEADS_B, TS, TQ), F32),
                        pltpu.VMEM((2, N_HEADS_B, 8, TQ), F32),
                        pltpu.VMEM((D_B, TQ), F32),
                        pltpu.VMEM((N_HEADS_B, 8, TQ), F32),
                        pltpu.VMEM((N_HEADS_B, 8, TQ), F32)],
        compiler_params=_cparams(("arbitrary",)),
        name="dsa",
    )(qbt, qit, wit, g_in, kb3, vt3, ki3, btab)


def _out_kernel(x_ref, ga_ref, gb_ref, a_ref, bq_ref, wa_ref, wb_ref, wo_ref, lg_ref, lb_ref,
                o_ref, o16_ref, *, alpha):
    ua = jnp.dot(a_ref[...], wa_ref[...], preferred_element_type=F32)
    ub = jnp.dot(bq_ref[...], wb_ref[...], preferred_element_type=F32)
    merged = jax.nn.sigmoid(ga_ref[...]) * ua + jax.nn.sigmoid(gb_ref[...]) * ub
    y = jnp.dot(merged.astype(BF16), wo_ref[...], preferred_element_type=F32)
    r = alpha * x_ref[...] + y
    mu = jnp.mean(r, axis=-1, keepdims=True)
    var = jnp.mean(jnp.square(r - mu), axis=-1, keepdims=True)
    out = (r - mu) * lax.rsqrt(var + LN_EPS) * lg_ref[...] + lb_ref[...]
    o_ref[...] = out
    o16_ref[...] = out.astype(BF16)


def _out_block(x, g_in, ga_in, gb_in, wa16, wb16, wo16, ln_g, ln_b, alpha, tm):
    L = x.shape[0]
    const = lambda shape: pl.BlockSpec(shape, lambda i: (0, 0))
    return pl.pallas_call(
        functools.partial(_out_kernel, alpha=alpha),
        grid=(L // tm,),
        in_specs=[pl.BlockSpec((tm, D_MODEL), lambda i: (i, 0)),
                  pl.BlockSpec((tm, D_MODEL), lambda i: (i, 1)),
                  pl.BlockSpec((tm, D_MODEL), lambda i: (i, 2)),
                  pl.BlockSpec((tm, D_A), lambda i: (i, 0)),
                  pl.BlockSpec((tm, D_B), lambda i: (i, 0)),
                  const((D_A, D_MODEL)), const((D_B, D_MODEL)), const((D_MODEL, D_MODEL)),
                  const((1, D_MODEL)), const((1, D_MODEL))],
        out_specs=[pl.BlockSpec((tm, D_MODEL), lambda i: (i, 0)),
                   pl.BlockSpec((tm, D_MODEL), lambda i: (i, 0))],
        out_shape=[jax.ShapeDtypeStruct((L, D_MODEL), F32),
                   jax.ShapeDtypeStruct((L, D_MODEL), BF16)],
        compiler_params=_cparams(("arbitrary",)),
        name="out_block",
    )(x, g_in, g_in, ga_in, gb_in, wa16, wb16, wo16, ln_g, ln_b)


def _layer(h, h16, w_in, b_in, w_up_a, w_up_b, w_out, lb, norm_g, btab, ln_g, ln_b, tril16,
           alpha, topk):
    L = h.shape[0]
    tm = min(512, L)
    c0 = 4 * D_A
    col = lambda a, n: (w_in[:, a:a + n], b_in[a:a + n])

    def normal(a, n, dtype, tn, scale=1.0):
        w, b = col(a, n)
        return _proj(h16, (w * scale).astype(BF16), (b * scale)[None, :], dtype, tm, tn)

    def transposed(a, n, dtype, blocked, scale=1.0, pad_to=None):
        w, b = col(a, n)
        wt = (w * scale).T
        bc = (b * scale)[:, None]
        if pad_to is not None:
            wt = jnp.pad(wt, ((0, pad_to - n), (0, 0)))
            bc = jnp.pad(bc, ((0, pad_to - n), (0, 0)))
        return _proj_t(wt.astype(BF16), h16, bc, dtype, KB if blocked else tm, blocked)

    a_in = normal(0, 3 * D_A, F32, 512)
    gparts = [col(3 * D_A, D_A), col(c0 + 3 * D_B, D_B), col(N_IN - 2 * D_MODEL, 2 * D_MODEL)]
    g_in = _proj(h16, jnp.concatenate([w for w, _ in gparts], axis=1).astype(BF16),
                 jnp.concatenate([b for _, b in gparts])[None, :], F32, tm, 512)
    qbt = transposed(c0, D_B, BF16, False, scale=HEAD_DIM_B ** -0.5 * LOG2E)
    kb3 = normal(c0 + D_B, D_B, BF16, 512).reshape(L // KB, KB, D_B)
    vt3 = transposed(c0 + 2 * D_B, D_B, BF16, True)
    ci = c0 + 4 * D_B
    qit = transposed(ci, IDX_HEADS * IDX_DIM, BF16, False, scale=IDX_DIM ** -0.5)
    ki3 = normal(ci + IDX_HEADS * IDX_DIM, IDX_DIM, BF16, IDX_DIM).reshape(L // KB, KB, IDX_DIM)
    wit = transposed(ci + IDX_HEADS * IDX_DIM + IDX_DIM, IDX_HEADS, F32, False, pad_to=16)

    ga_in = _hgrn(a_in, g_in, lb[None, :], norm_g[None, :], tril16)
    gb_in = _dsa(qbt, qit, wit, g_in, kb3, vt3, ki3, btab, topk)
    return _out_block(h, g_in, ga_in, gb_in, w_up_a.astype(BF16), w_up_b.astype(BF16),
                      w_out.astype(BF16), ln_g[None, :], ln_b[None, :], alpha, tm)


N_IN = 4 * D_A + 4 * D_B + IDX_HEADS * IDX_DIM + IDX_DIM + IDX_HEADS + 2 * D_MODEL


def kernel(x, w_in, b_in, w_up_a, w_up_b, w_out, lb_logits, norm_a_g, rel_bias, ln_g, ln_b):
    depth = w_in.shape[0]
    batch, L, _ = x.shape
    alpha = (2 * depth) ** 0.25
    topk = min(TOPK_MAX, L // 4)
    lbs = jnp.cumsum(jax.nn.softmax(lb_logits.astype(F32), axis=0), axis=0)
    lbs = lbs - lbs[0:1]
    r = jnp.arange(HG_ROWS)
    tril16 = ((r[:, None] >= r[None, :]) & ((r[:, None] // HG_CHUNK) == (r[None, :] // HG_CHUNK))).astype(BF16)
    btab = _bias_table(rel_bias.astype(F32))
    outs = []
    for bi in range(batch):
        h = x[bi]
        h16 = h.astype(BF16)
        for layer in range(depth):
            h, h16 = _layer(h, h16, w_in[layer], b_in[layer], w_up_a[layer], w_up_b[layer], w_out[layer],
                            lbs[layer], norm_a_g[layer], btab, ln_g[layer], ln_b[layer], tril16, alpha, topk)
        outs.append(h)
    return jnp.stack(outs, axis=0)
```

```python
import functools
import math

import jax
import jax.numpy as jnp
from jax import lax
from jax.experimental import pallas as pl
from jax.experimental.pallas import tpu as pltpu

F32 = jnp.float32
BF16 = jnp.bfloat16
I32 = jnp.int32

D_MODEL = 1024
D_A = 512
HEAD_DIM_A = 128
N_HEADS_A = D_A // HEAD_DIM_A
N_HEADS_B = 8
HEAD_DIM_B = 64
D_B = N_HEADS_B * HEAD_DIM_B
IDX_HEADS = 8
IDX_DIM = 64
TOPK_MAX = 256
N_BUCKETS = 32
MAX_DISTANCE = 128
LN_EPS = 1e-5
RMS_EPS = 1e-6

LANES = 128
VMEM_LIMIT_BYTES = 58 * 1024 * 1024

HG_ROWS = 512
HG_CHUNK = 64
HG_SUB = 16

TQ = 128
KB = 128
TS = 512
TF = 1024
NEAR = 256
MASK_NEG = -1e30
CAND_G = 16
CAND_M = 12
CAND_MIN_TILES = 9
LOG2E = 1.4426950408889634

INT_MIN = -2147483648
NEG_INF_KEY = -2139095041


def _cparams(sem):
    return pltpu.CompilerParams(dimension_semantics=sem, vmem_limit_bytes=VMEM_LIMIT_BYTES)


def _proj_kernel(x_ref, w_ref, b_ref, o_ref):
    acc = jnp.dot(x_ref[...], w_ref[...], preferred_element_type=F32)
    o_ref[...] = (acc + b_ref[...]).astype(o_ref.dtype)


def _proj(x16, w16, b, out_dtype, tm, tn):
    m, k = x16.shape
    n = w16.shape[1]
    return pl.pallas_call(
        _proj_kernel,
        grid=(n // tn, m // tm),
        in_specs=[pl.BlockSpec((tm, k), lambda j, i: (i, 0)),
                  pl.BlockSpec((k, tn), lambda j, i: (0, j)),
                  pl.BlockSpec((1, tn), lambda j, i: (0, j))],
        out_specs=pl.BlockSpec((tm, tn), lambda j, i: (i, j)),
        out_shape=jax.ShapeDtypeStruct((m, n), out_dtype),
        compiler_params=_cparams(("arbitrary", "arbitrary")),
        name="proj",
    )(x16, w16, b)


def _proj_t_kernel(wt_ref, x_ref, b_ref, o_ref):
    acc = lax.dot_general(wt_ref[...], x_ref[...], (((1,), (1,)), ((), ())),
                          preferred_element_type=F32)
    o_ref[...] = (acc + b_ref[...]).astype(o_ref.dtype).reshape(o_ref.shape)


def _proj_t(wt16, x16, bcol, out_dtype, tm, blocked):
    m, k = x16.shape
    n = wt16.shape[0]
    if blocked:
        out_shape = jax.ShapeDtypeStruct((m // tm, n, tm), out_dtype)
        out_spec = pl.BlockSpec((1, n, tm), lambda i: (i, 0, 0))
    else:
        out_shape = jax.ShapeDtypeStruct((n, m), out_dtype)
        out_spec = pl.BlockSpec((n, tm), lambda i: (0, i))
    return pl.pallas_call(
        _proj_t_kernel,
        grid=(m // tm,),
        in_specs=[pl.BlockSpec((n, k), lambda i: (0, 0)),
                  pl.BlockSpec((tm, k), lambda i: (i, 0)),
                  pl.BlockSpec((n, 1), lambda i: (0, 0))],
        out_specs=out_spec,
        out_shape=out_shape,
        compiler_params=_cparams(("arbitrary",)),
        name="proj_t",
    )(wt16, x16, bcol)


def _split3_dot(t16, x):
    x1 = x.astype(BF16)
    r1 = x - x1.astype(F32)
    x2 = r1.astype(BF16)
    r2 = r1 - x2.astype(F32)
    x3 = r2.astype(BF16)
    return (jnp.dot(t16, x1, preferred_element_type=F32)
            + jnp.dot(t16, x2, preferred_element_type=F32)
            + jnp.dot(t16, x3, preferred_element_type=F32))


def _hgrn_kernel(lb_ref, g_ref, q_ref, f_ref, i_ref, z_ref, tril_ref, o_ref, st_ref):
    @pl.when(pl.program_id(1) == 0)
    def _():
        st_ref[...] = jnp.zeros_like(st_ref)

    rows = q_ref.shape[0]
    q = q_ref[...]
    fl = f_ref[...]
    v = i_ref[...]
    lb = lb_ref[...]
    log_lb = jnp.log(lb)
    log_1m = jnp.log(1.0 - lb)
    log_sig = jnp.minimum(fl, 0.0) - jnp.log(1.0 + jnp.exp(-jnp.abs(fl)))
    bb = log_1m + log_sig
    mx = jnp.maximum(log_lb, bb)
    log_f = mx + jnp.log(1.0 + jnp.exp(-jnp.abs(log_lb - bb)))
    kk = (1.0 - lb) * jax.nn.sigmoid(-fl)
    b = _split3_dot(tril_ref[...], log_f)

    row_l = lax.broadcasted_iota(I32, (rows, 1), 0) % HG_SUB
    ones16 = jnp.ones((LANES, LANES), BF16)
    qk0 = (q * kk).astype(BF16)
    o = jnp.dot(qk0, ones16, preferred_element_type=F32) * v
    for d in range(1, HG_SUB):
        valid = (row_l + d) < HG_SUB
        qd = pltpu.roll(q, rows - d, 0)
        bd = pltpu.roll(b, rows - d, 0)
        dec = jnp.exp(jnp.where(valid, bd - b, 0.0))
        e = jnp.where(valid, qd * kk * dec, 0.0).astype(BF16)
        c = jnp.dot(e, ones16, preferred_element_type=F32) * v
        o = o + pltpu.roll(c, d, 0)

    tl = lax.broadcasted_iota(I32, (HG_CHUNK, HG_CHUNK), 0)
    sl = lax.broadcasted_iota(I32, (HG_CHUNK, HG_CHUNK), 1)
    mask1 = ((tl // 32) == (sl // 32)) & ((tl % 32) >= 16) & ((sl % 32) < 16)
    rl = lax.broadcasted_iota(I32, (HG_CHUNK, 1), 0)
    hi32 = rl >= 32
    hi16 = (rl % 32) >= 16
    grp1 = rl >= 32

    st = st_ref[...]
    outs = []
    for c in range(rows // HG_CHUNK):
        sl_c = slice(c * HG_CHUNK, (c + 1) * HG_CHUNK)
        qc, kc, vc, bc = q[sl_c], kk[sl_c], v[sl_c], b[sl_c]
        b_last = bc[HG_CHUNK - 1:HG_CHUNK, :]
        r2 = bc[31:32, :]
        r1 = jnp.where(grp1, bc[47:48, :], bc[15:16, :])
        q2 = jnp.where(hi32, qc * jnp.exp(jnp.where(hi32, bc - r2, 0.0)), 0.0)
        k2 = jnp.where(hi32, 0.0, kc * jnp.exp(jnp.where(hi32, 0.0, r2 - bc)))
        q1 = jnp.where(hi16, qc * jnp.exp(jnp.where(hi16, bc - r1, 0.0)), 0.0)
        k1 = jnp.where(hi16, 0.0, kc * jnp.exp(jnp.where(hi16, 0.0, r1 - bc)))
        s2 = lax.dot_general(q2.astype(BF16), k2.astype(BF16), (((1,), (1,)), ((), ())),
                             preferred_element_type=F32)
        s1 = lax.dot_general(q1.astype(BF16), k1.astype(BF16), (((1,), (1,)), ((), ())),
                             preferred_element_type=F32)
        p = s2 + jnp.where(mask1, s1, 0.0)
        o_c = jnp.dot(p.astype(BF16), vc.astype(BF16), preferred_element_type=F32)
        qe = (qc * jnp.exp(bc)).astype(BF16)
        o_c = o_c + lax.dot_general(qe, st.astype(BF16), (((1,), (1,)), ((), ())),
                                    preferred_element_type=F32)
        ke = (kc * jnp.exp(b_last - bc)).astype(BF16)
        upd = lax.dot_general(vc.astype(BF16), ke, (((0,), (0,)), ((), ())),
                              preferred_element_type=F32)
        st = st * jnp.exp(b_last) + upd
        outs.append(o_c)
    st_ref[...] = st
    o = o + jnp.concatenate(outs, axis=0)

    ms = jnp.mean(o * o, axis=-1, keepdims=True)
    oa = o * lax.rsqrt(ms + RMS_EPS) * g_ref[...]
    z = z_ref[...]
    o_ref[...] = (oa * (z * jax.nn.sigmoid(z))).astype(o_ref.dtype)


def _hgrn(a_in, g_in, lb, norm_g, tril16):
    L = a_in.shape[0]
    nh = N_HEADS_A
    blk = lambda off: pl.BlockSpec((HG_ROWS, HEAD_DIM_A), lambda h, s: (s, off + h))
    return pl.pallas_call(
        _hgrn_kernel,
        grid=(nh, L // HG_ROWS),
        in_specs=[pl.BlockSpec((1, HEAD_DIM_A), lambda h, s: (0, h)),
                  pl.BlockSpec((1, HEAD_DIM_A), lambda h, s: (0, h)),
                  blk(0), blk(nh), blk(2 * nh),
                  pl.BlockSpec((HG_ROWS, HEAD_DIM_A), lambda h, s: (s, h)),
                  pl.BlockSpec((HG_ROWS, HG_ROWS), lambda h, s: (0, 0))],
        out_specs=pl.BlockSpec((HG_ROWS, HEAD_DIM_A), lambda h, s: (s, h)),
        out_shape=jax.ShapeDtypeStruct((L, D_A), BF16),
        scratch_shapes=[pltpu.VMEM((HEAD_DIM_A, HEAD_DIM_A), F32)],
        compiler_params=_cparams(("arbitrary", "arbitrary")),
        name="hgrn2",
    )(lb, norm_g, a_in, a_in, a_in, g_in, tril16)


def _bias_table_kernel(rb_ref, o_ref):
    n_c = o_ref.shape[1]
    c = lax.broadcasted_iota(I32, (n_c, TQ), 0)
    t = lax.broadcasted_iota(I32, (n_c, TQ), 1)
    dist = t + KB - c
    max_exact = N_BUCKETS // 2
    d = jnp.maximum(dist, 0)
    df = jnp.maximum(d, 1).astype(F32)
    large = max_exact + (jnp.log(df / max_exact) / math.log(MAX_DISTANCE / max_exact)
                         * (N_BUCKETS - max_exact)).astype(I32)
    large = jnp.minimum(large, N_BUCKETS - 1)
    bucket = jnp.where(d < max_exact, d, large)
    for h in range(N_HEADS_B):
        acc = jnp.zeros((n_c, TQ), F32)
        for k in range(N_BUCKETS):
            acc = jnp.where(bucket == k, rb_ref[k, h], acc)
        o_ref[h] = (acc - rb_ref[N_BUCKETS - 1, h]) * LOG2E


def _bias_table(rel_bias):
    return pl.pallas_call(
        _bias_table_kernel,
        in_specs=[pl.BlockSpec(memory_space=pltpu.SMEM)],
        out_specs=pl.BlockSpec(memory_space=pltpu.VMEM),
        out_shape=jax.ShapeDtypeStruct((N_HEADS_B, NEAR + KB, TQ), F32),
        name="t5_bias_table",
    )(rel_bias)


def _dsa_kernel(qbt_ref, qit_ref, wit_ref, zb_ref, kb_ref, vt_ref, ki_ref, bt_ref, o_ref,
                key_ref, cand_ref, candk_ref, stat_ref, lg_ref, mt_ref, acc_ref, m_ref, l_ref,
                *, topk, idx_bits):
    i = pl.program_id(0)
    t_idx = i * TQ + lax.broadcasted_iota(I32, (1, TQ), 1)
    n_kb = TS // KB
    n_fb = TF // KB
    nt = (TF // TS) * ((i + n_fb) // n_fb)
    row_kb = lax.broadcasted_iota(I32, (KB, 1), 0)
    row_ts = lax.broadcasted_iota(I32, (TS, 1), 0)

    qit = qit_ref[...]
    w = wit_ref[...] * (IDX_HEADS ** -0.5)
    qi_pairs = [jnp.concatenate([qit[2 * p * IDX_DIM:(2 * p + 1) * IDX_DIM, :],
                                 qit[(2 * p + 1) * IDX_DIM:(2 * p + 2) * IDX_DIM, :]], axis=1)
                for p in range(IDX_HEADS // 2)]

    def score_tile(j, carry):
        kt = ki_ref[pl.ds(j * n_kb, n_kb)].reshape(TS, IDX_DIM)
        acc = jnp.zeros((TS, TQ), F32)
        for p in range(IDX_HEADS // 2):
            dd = jnp.dot(kt, qi_pairs[p], preferred_element_type=F32)
            acc = acc + w[2 * p:2 * p + 1, :] * jnp.maximum(dd[:, :TQ], 0.0)
            acc = acc + w[2 * p + 1:2 * p + 2, :] * jnp.maximum(dd[:, TQ:], 0.0)
        s_idx = j * TS + row_ts
        sc = jnp.where(s_idx <= t_idx, acc, -jnp.inf)
        sc = sc + 0.0
        bits = pltpu.bitcast(sc, I32)
        keys = bits ^ ((bits >> 31) & 0x7FFFFFFF)
        key_ref[pl.ds(j * n_kb, n_kb)] = keys.reshape(n_kb, KB, TQ)
        sc3 = sc.reshape(TS // 8, 8, TQ)
        for g in range(CAND_G):
            lst = [cand_ref[g, k] for k in range(CAND_M)]
            for c in range(TS // 8 // CAND_G):
                x = sc3[g + CAND_G * c]
                for k in range(CAND_M):
                    hi = jnp.maximum(lst[k], x)
                    x = jnp.minimum(lst[k], x)
                    lst[k] = hi
            for k in range(CAND_M):
                cand_ref[g, k] = lst[k]
        return carry

    cand_ref[...] = jnp.full(cand_ref.shape, -jnp.inf, F32)
    lax.fori_loop(0, nt, score_tile, 0)

    def to_keys(sc):
        bits = pltpu.bitcast(sc, I32)
        return bits ^ ((bits >> 31) & 0x7FFFFFFF)

    def count_ge(thr_signed):
        def body(j, cnt):
            keys = key_ref[pl.ds(j * n_kb, n_kb)].reshape(TS // 8, 8, TQ)
            return cnt + jnp.sum((keys >= thr_signed).astype(I32), axis=0)
        cnt = lax.fori_loop(0, nt, body, jnp.zeros((8, TQ), I32))
        return jnp.sum(cnt, axis=0, keepdims=True)

    def bisect(count_fn):
        def bit_step(bi, tb):
            cand = tb | jnp.left_shift(jnp.int32(1), 31 - bi)
            return jnp.where(count_fn(cand ^ INT_MIN) >= topk, cand, tb)
        tb = lax.fori_loop(0, 32, bit_step, jnp.zeros((1, TQ), I32))
        return jnp.maximum(tb ^ INT_MIN, NEG_INF_KEY + 1)

    stat_ref[2] = jnp.ones((8, TQ), I32)

    @pl.when(nt > CAND_MIN_TILES)
    def _():
        candk_ref[...] = to_keys(cand_ref[...].reshape(CAND_G * CAND_M, 8, TQ))

        def count_cand_ge(thr_signed):
            hit = (candk_ref[...] >= thr_signed).astype(I32)
            return jnp.sum(jnp.sum(hit, axis=0), axis=0, keepdims=True)

        thr_c = bisect(count_cand_ge)
        stat_ref[0] = jnp.broadcast_to(thr_c, (8, TQ))
        stat_ref[1] = jnp.broadcast_to(count_ge(thr_c), (8, TQ))
        stat_ref[2] = jnp.broadcast_to((count_ge(thr_c + 1) >= topk).astype(I32), (8, TQ))

    @pl.when(jnp.max(stat_ref[2]) > 0)
    def _():
        thr_f = bisect(count_ge)
        stat_ref[0] = jnp.broadcast_to(thr_f, (8, TQ))
        stat_ref[1] = jnp.broadcast_to(count_ge(thr_f), (8, TQ))

    thr = stat_ref[0, 0:1, :]
    cnt_ge_thr = stat_ref[1, 0:1, :]
    excess = (cnt_ge_thr > topk)

    @pl.when(jnp.max(excess.astype(I32)) > 0)
    def _():
        cnt_gt = count_ge(thr + 1)
        need = topk - cnt_gt

        def count_eq_below(cut):
            def body(j, cnt):
                keys = key_ref[pl.ds(j * n_kb, n_kb)].reshape(TS, TQ)
                s_idx = j * TS + row_ts
                hit = (keys == thr) & (s_idx < cut)
                return cnt + jnp.sum(hit.astype(I32).reshape(TS // 8, 8, TQ), axis=0)
            cnt = lax.fori_loop(0, nt, body, jnp.zeros((8, TQ), I32))
            return jnp.sum(cnt, axis=0, keepdims=True)

        def idx_step(bi, cut):
            cand = cut | jnp.left_shift(jnp.int32(1), idx_bits - 1 - bi)
            return jnp.where(count_eq_below(cand) < need, cand, cut)

        cut = lax.fori_loop(0, idx_bits, idx_step, jnp.zeros((1, TQ), I32))
        def demote(j, carry):
            keys = key_ref[pl.ds(j * n_kb, n_kb)].reshape(TS, TQ)
            s_idx = j * TS + row_ts
            drop = excess & (keys == thr) & (s_idx > cut)
            key_ref[pl.ds(j * n_kb, n_kb)] = jnp.where(drop, keys - 1, keys).reshape(n_kb, KB, TQ)
            return carry
        lax.fori_loop(0, nt, demote, 0)

    qbt = qbt_ref[...]
    zq = jnp.zeros((HEAD_DIM_B, TQ), BF16)
    qb_pairs = [jnp.concatenate(
        [jnp.concatenate([qbt[2 * p * HEAD_DIM_B:(2 * p + 1) * HEAD_DIM_B, :], zq], axis=1),
         jnp.concatenate([zq, qbt[(2 * p + 1) * HEAD_DIM_B:(2 * p + 2) * HEAD_DIM_B, :]], axis=1)],
        axis=0) for p in range(N_HEADS_B // 2)]

    m_ref[...] = jnp.full(m_ref.shape, MASK_NEG, F32)
    l_ref[...] = jnp.zeros(l_ref.shape, F32)
    acc_ref[...] = jnp.zeros(acc_ref.shape, F32)

    def stage1(slot, kt, madd, bias_of_head):
        n = kt.shape[0]
        for p in range(N_HEADS_B // 2):
            lg = jnp.dot(kt[:, 2 * p * HEAD_DIM_B:(2 * p + 2) * HEAD_DIM_B], qb_pairs[p],
                         preferred_element_type=F32)
            for hh in range(2):
                h = 2 * p + hh
                lo = lg[:, hh * TQ:(hh + 1) * TQ] + madd
                if bias_of_head is not None:
                    lo = lo + bias_of_head(h)
                lg_ref[slot, h, 0:n, :] = lo
                mt_ref[slot, h] = jnp.broadcast_to(jnp.max(lo, axis=0, keepdims=True), (8, TQ))

    def stage2(slot, vt):
        n = vt.shape[1]
        for h in range(N_HEADS_B):
            m_old = m_ref[h]
            m_new = jnp.maximum(m_old, mt_ref[slot, h])
            alpha = jnp.exp2(m_old - m_new)
            m_ref[h] = m_new
            pe = jnp.exp2(lg_ref[slot, h, 0:n, :] - m_new[0:1, :]).astype(BF16)
            rs = slice(h * HEAD_DIM_B, (h + 1) * HEAD_DIM_B)
            vt_aug = jnp.concatenate([vt[rs, :], jnp.ones((16, n), BF16)], axis=0)
            pv = jnp.dot(vt_aug, pe, preferred_element_type=F32)
            l_ref[h] = alpha * l_ref[h] + pv[HEAD_DIM_B:HEAD_DIM_B + 8, :]
            acc_ref[rs, :] = alpha[0:1, :] * acc_ref[rs, :] + pv[0:HEAD_DIM_B, :]

    far_end = (i - 1) * KB
    nf = (i + n_fb - 2) // n_fb
    row_tf = lax.broadcasted_iota(I32, (TF, 1), 0)

    def far_stage1(j):
        kt = kb_ref[pl.ds(j * n_fb, n_fb)].reshape(TF, D_B)
        keys = key_ref[pl.ds(j * n_fb, n_fb)].reshape(TF, TQ)
        s_idx = j * TF + row_tf
        madd = jnp.where((keys >= thr) & (s_idx < far_end), 0.0, MASK_NEG)
        stage1(0, kt, madd, None)

    def far_stage2(j):
        vt = jnp.concatenate([vt_ref[j * n_fb + c] for c in range(n_fb)], axis=1)
        stage2(0, vt)

    nb0 = jnp.maximum(i - 1, 0)

    def near_stage1(slot):
        toff = jnp.where(i == 0, KB, 0)
        kt = kb_ref[pl.ds(nb0, NEAR // KB)].reshape(NEAR, D_B)
        keys = key_ref[pl.ds(nb0, NEAR // KB)].reshape(NEAR, TQ)
        madd = jnp.where(keys >= thr, 0.0, MASK_NEG)
        stage1(slot, kt, madd, lambda h: bt_ref[h, pl.ds(pl.multiple_of(toff, KB), NEAR), :])

    def far_tile(j, carry):
        far_stage1(j)
        far_stage2(j)
        return carry
    lax.fori_loop(0, nf, far_tile, 0)
    near_stage1(0)
    stage2(0, jnp.concatenate([vt_ref[nb0 + c] for c in range(NEAR // KB)], axis=1))

    ot = jnp.concatenate([acc_ref[h * HEAD_DIM_B:(h + 1) * HEAD_DIM_B, :] * (1.0 / l_ref[h, 0:1, :])
                          for h in range(N_HEADS_B)], axis=0)
    ob = ot.T
    z = zb_ref[...]
    o_ref[...] = (ob * (z * jax.nn.sigmoid(z))).astype(o_ref.dtype)


def _dsa(qbt, qit, wit, g_in, kb3, vt3, ki3, btab, topk):
    L = qbt.shape[1]
    n_blk = L // KB
    idx_bits = max(1, (L - 1).bit_length())
    whole = pl.BlockSpec(memory_space=pltpu.VMEM)
    return pl.pallas_call(
        functools.partial(_dsa_kernel, topk=topk, idx_bits=idx_bits),
        grid=(L // TQ,),
        in_specs=[pl.BlockSpec((D_B, TQ), lambda i: (0, i)),
                  pl.BlockSpec((IDX_HEADS * IDX_DIM, TQ), lambda i: (0, i)),
                  pl.BlockSpec((16, TQ), lambda i: (0, i)),
                  pl.BlockSpec((TQ, D_B), lambda i: (i, 1)),
                  whole, whole, whole, whole],
        out_specs=pl.BlockSpec((TQ, D_B), lambda i: (i, 0)),
        out_shape=jax.ShapeDtypeStruct((L, D_B), BF16),
        scratch_shapes=[pltpu.VMEM((n_blk, KB, TQ), I32),
                        pltpu.VMEM((CAND_G, CAND_M, 8, TQ), F32),
                        pltpu.VMEM((CAND_G * CAND_M, 8, TQ), I32),
                        pltpu.VMEM((3, 8, TQ), I32),
                        pltpu.VMEM((1, N_HEADS_B, TF, TQ), F32),
                        pltpu.VMEM((1, N_HEADS_B, 8, TQ), F32),
                        pltpu.VMEM((D_B, TQ), F32),
                        pltpu.VMEM((N_HEADS_B, 8, TQ), F32),
                        pltpu.VMEM((N_HEADS_B, 8, TQ), F32)],
        compiler_params=_cparams(("arbitrary",)),
        name="dsa",
    )(qbt, qit, wit, g_in, kb3, vt3, ki3, btab)


def _out_kernel(x_ref, ga_ref, gb_ref, a_ref, bq_ref, wa_ref, wb_ref, wo_ref, lg_ref, lb_ref,
                o_ref, o16_ref, *, alpha):
    ua = jnp.dot(a_ref[...], wa_ref[...], preferred_element_type=F32)
    ub = jnp.dot(bq_ref[...], wb_ref[...], preferred_element_type=F32)
    merged = jax.nn.sigmoid(ga_ref[...]) * ua + jax.nn.sigmoid(gb_ref[...]) * ub
    y = jnp.dot(merged.astype(BF16), wo_ref[...], preferred_element_type=F32)
    r = alpha * x_ref[...] + y
    mu = jnp.mean(r, axis=-1, keepdims=True)
    var = jnp.mean(jnp.square(r - mu), axis=-1, keepdims=True)
    out = (r - mu) * lax.rsqrt(var + LN_EPS) * lg_ref[...] + lb_ref[...]
    o_ref[...] = out
    o16_ref[...] = out.astype(BF16)


def _out_block(x, g_in, ga_in, gb_in, wa16, wb16, wo16, ln_g, ln_b, alpha, tm):
    L = x.shape[0]
    const = lambda shape: pl.BlockSpec(shape, lambda i: (0, 0))
    return pl.pallas_call(
        functools.partial(_out_kernel, alpha=alpha),
        grid=(L // tm,),
        in_specs=[pl.BlockSpec((tm, D_MODEL), lambda i: (i, 0)),
                  pl.BlockSpec((tm, D_MODEL), lambda i: (i, 1)),
                  pl.BlockSpec((tm, D_MODEL), lambda i: (i, 2)),
                  pl.BlockSpec((tm, D_A), lambda i: (i, 0)),
                  pl.BlockSpec((tm, D_B), lambda i: (i, 0)),
                  const((D_A, D_MODEL)), const((D_B, D_MODEL)), const((D_MODEL, D_MODEL)),
                  const((1, D_MODEL)), const((1, D_MODEL))],
        out_specs=[pl.BlockSpec((tm, D_MODEL), lambda i: (i, 0)),
                   pl.BlockSpec((tm, D_MODEL), lambda i: (i, 0))],
        out_shape=[jax.ShapeDtypeStruct((L, D_MODEL), F32),
                   jax.ShapeDtypeStruct((L, D_MODEL), BF16)],
        compiler_params=_cparams(("arbitrary",)),
        name="out_block",
    )(x, g_in, g_in, ga_in, gb_in, wa16, wb16, wo16, ln_g, ln_b)


def _layer(h, h16, w_in, b_in, w_up_a, w_up_b, w_out, lb, norm_g, btab, ln_g, ln_b, tril16,
           alpha, topk):
    L = h.shape[0]
    tm = min(512, L)
    c0 = 4 * D_A
    col = lambda a, n: (w_in[:, a:a + n], b_in[a:a + n])

    def normal(a, n, dtype, tn, scale=1.0):
        w, b = col(a, n)
        return _proj(h16, (w * scale).astype(BF16), (b * scale)[None, :], dtype, tm, tn)

    def transposed(a, n, dtype, blocked, scale=1.0, pad_to=None):
        w, b = col(a, n)
        wt = (w * scale).T
        bc = (b * scale)[:, None]
        if pad_to is not None:
            wt = jnp.pad(wt, ((0, pad_to - n), (0, 0)))
            bc = jnp.pad(bc, ((0, pad_to - n), (0, 0)))
        return _proj_t(wt.astype(BF16), h16, bc, dtype, KB if blocked else tm, blocked)

    a_in = normal(0, 3 * D_A, F32, 512)
    gparts = [col(3 * D_A, D_A), col(c0 + 3 * D_B, D_B), col(N_IN - 2 * D_MODEL, 2 * D_MODEL)]
    g_in = _proj(h16, jnp.concatenate([w for w, _ in gparts], axis=1).astype(BF16),
                 jnp.concatenate([b for _, b in gparts])[None, :], F32, tm, 512)
    qbt = transposed(c0, D_B, BF16, False, scale=HEAD_DIM_B ** -0.5 * LOG2E)
    kb3 = normal(c0 + D_B, D_B, BF16, 512).reshape(L // KB, KB, D_B)
    vt3 = transposed(c0 + 2 * D_B, D_B, BF16, True)
    ci = c0 + 4 * D_B
    qit = transposed(ci, IDX_HEADS * IDX_DIM, BF16, False, scale=IDX_DIM ** -0.5)
    ki3 = normal(ci + IDX_HEADS * IDX_DIM, IDX_DIM, BF16, IDX_DIM).reshape(L // KB, KB, IDX_DIM)
    wit = transposed(ci + IDX_HEADS * IDX_DIM + IDX_DIM, IDX_HEADS, F32, False, pad_to=16)

    ga_in = _hgrn(a_in, g_in, lb[None, :], norm_g[None, :], tril16)
    gb_in = _dsa(qbt, qit, wit, g_in, kb3, vt3, ki3, btab, topk)
    return _out_block(h, g_in, ga_in, gb_in, w_up_a.astype(BF16), w_up_b.astype(BF16),
                      w_out.astype(BF16), ln_g[None, :], ln_b[None, :], alpha, tm)


N_IN = 4 * D_A + 4 * D_B + IDX_HEADS * IDX_DIM + IDX_DIM + IDX_HEADS + 2 * D_MODEL


def kernel(x, w_in, b_in, w_up_a, w_up_b, w_out, lb_logits, norm_a_g, rel_bias, ln_g, ln_b):
    depth = w_in.shape[0]
    batch, L, _ = x.shape
    alpha = (2 * depth) ** 0.25
    topk = min(TOPK_MAX, L // 4)
    lbs = jnp.cumsum(jax.nn.softmax(lb_logits.astype(F32), axis=0), axis=0)
    lbs = lbs - lbs[0:1]
    r = jnp.arange(HG_ROWS)
    tril16 = ((r[:, None] >= r[None, :]) & ((r[:, None] // HG_CHUNK) == (r[None, :] // HG_CHUNK))).astype(BF16)
    btab = _bias_table(rel_bias.astype(F32))
    outs = []
    for bi in range(batch):
        h = x[bi]
        h16 = h.astype(BF16)
        for layer in range(depth):
            h, h16 = _layer(h, h16, w_in[layer], b_in[layer], w_up_a[layer], w_up_b[layer], w_out[layer],
                            lbs[layer], norm_a_g[layer], btab, ln_g[layer], ln_b[layer], tril16, alpha, topk)
        outs.append(h)
    return jnp.stack(outs, axis=0)
```

```python
import functools
import math

import jax
import jax.numpy as jnp
from jax import lax
from jax.experimental import pallas as pl
from jax.experimental.pallas import tpu as pltpu

F32 = jnp.float32
BF16 = jnp.bfloat16
I32 = jnp.int32

D_MODEL = 1024
D_A = 512
HEAD_DIM_A = 128
N_HEADS_A = D_A // HEAD_DIM_A
N_HEADS_B = 8
HEAD_DIM_B = 64
D_B = N_HEADS_B * HEAD_DIM_B
IDX_HEADS = 8
IDX_DIM = 64
TOPK_MAX = 256
N_BUCKETS = 32
MAX_DISTANCE = 128
LN_EPS = 1e-5
RMS_EPS = 1e-6

LANES = 128
VMEM_LIMIT_BYTES = 58 * 1024 * 1024

HG_ROWS = 512
HG_CHUNK = 64
HG_SUB = 16

TQ = 128
KB = 128
TS = 1024
TF = 1024
NEAR = 256
MASK_NEG = -1e30
CAND_G = 16
CAND_M = 12
CAND_MIN_TILES = 2
LOG2E = 1.4426950408889634

INT_MIN = -2147483648
NEG_INF_KEY = -2139095041


def _cparams(sem):
    return pltpu.CompilerParams(dimension_semantics=sem, vmem_limit_bytes=VMEM_LIMIT_BYTES)


def _tree_sum(xs):
    xs = list(xs)
    while len(xs) > 1:
        nxt = [xs[a] + xs[a + 1] for a in range(0, len(xs) - 1, 2)]
        if len(xs) % 2:
            nxt.append(xs[-1])
        xs = nxt
    return xs[0]


def _proj_kernel(x_ref, w_ref, b_ref, o_ref):
    acc = jnp.dot(x_ref[...], w_ref[...], preferred_element_type=F32)
    o_ref[...] = (acc + b_ref[...]).astype(o_ref.dtype)


def _proj(x16, w16, b, out_dtype, tm, tn):
    m, k = x16.shape
    n = w16.shape[1]
    return pl.pallas_call(
        _proj_kernel,
        grid=(n // tn, m // tm),
        in_specs=[pl.BlockSpec((tm, k), lambda j, i: (i, 0)),
                  pl.BlockSpec((k, tn), lambda j, i: (0, j)),
                  pl.BlockSpec((1, tn), lambda j, i: (0, j))],
        out_specs=pl.BlockSpec((tm, tn), lambda j, i: (i, j)),
        out_shape=jax.ShapeDtypeStruct((m, n), out_dtype),
        compiler_params=_cparams(("arbitrary", "arbitrary")),
        name="proj",
    )(x16, w16, b)


def _proj_t_kernel(wt_ref, x_ref, b_ref, o_ref):
    acc = lax.dot_general(wt_ref[...], x_ref[...], (((1,), (1,)), ((), ())),
                          preferred_element_type=F32)
    o_ref[...] = (acc + b_ref[...]).astype(o_ref.dtype).reshape(o_ref.shape)


def _proj_t(wt16, x16, bcol, out_dtype, tm, blocked):
    m, k = x16.shape
    n = wt16.shape[0]
    if blocked:
        out_shape = jax.ShapeDtypeStruct((m // tm, n, tm), out_dtype)
        out_spec = pl.BlockSpec((1, n, tm), lambda i: (i, 0, 0))
    else:
        out_shape = jax.ShapeDtypeStruct((n, m), out_dtype)
        out_spec = pl.BlockSpec((n, tm), lambda i: (0, i))
    return pl.pallas_call(
        _proj_t_kernel,
        grid=(m // tm,),
        in_specs=[pl.BlockSpec((n, k), lambda i: (0, 0)),
                  pl.BlockSpec((tm, k), lambda i: (i, 0)),
                  pl.BlockSpec((n, 1), lambda i: (0, 0))],
        out_specs=out_spec,
        out_shape=out_shape,
        compiler_params=_cparams(("arbitrary",)),
        name="proj_t",
    )(wt16, x16, bcol)


def _split3_dot(t16, x):
    x1 = x.astype(BF16)
    r1 = x - x1.astype(F32)
    x2 = r1.astype(BF16)
    r2 = r1 - x2.astype(F32)
    x3 = r2.astype(BF16)
    return (jnp.dot(t16, x1, preferred_element_type=F32)
            + jnp.dot(t16, x2, preferred_element_type=F32)
            + jnp.dot(t16, x3, preferred_element_type=F32))


def _hgrn_kernel(lb_ref, g_ref, q_ref, f_ref, i_ref, z_ref, tril_ref, o_ref, st_ref):
    @pl.when(pl.program_id(1) == 0)
    def _():
        st_ref[...] = jnp.zeros_like(st_ref)

    rows = q_ref.shape[0]
    q = q_ref[...]
    fl = f_ref[...]
    v = i_ref[...]
    lb = lb_ref[...]
    log_lb = jnp.log(lb)
    log_1m = jnp.log(1.0 - lb)
    log_sig = jnp.minimum(fl, 0.0) - jnp.log(1.0 + jnp.exp(-jnp.abs(fl)))
    bb = log_1m + log_sig
    mx = jnp.maximum(log_lb, bb)
    log_f = mx + jnp.log(1.0 + jnp.exp(-jnp.abs(log_lb - bb)))
    kk = (1.0 - lb) * jax.nn.sigmoid(-fl)
    b = _split3_dot(tril_ref[...], log_f)

    row_l = lax.broadcasted_iota(I32, (rows, 1), 0) % HG_SUB
    ones16 = jnp.ones((LANES, LANES), BF16)
    qk0 = (q * kk).astype(BF16)
    o = jnp.dot(qk0, ones16, preferred_element_type=F32) * v
    for d in range(1, HG_SUB):
        valid = (row_l + d) < HG_SUB
        qd = pltpu.roll(q, rows - d, 0)
        bd = pltpu.roll(b, rows - d, 0)
        dec = jnp.exp(jnp.where(valid, bd - b, 0.0))
        e = jnp.where(valid, qd * kk * dec, 0.0).astype(BF16)
        c = jnp.dot(e, ones16, preferred_element_type=F32) * v
        o = o + pltpu.roll(c, d, 0)

    tl = lax.broadcasted_iota(I32, (HG_CHUNK, HG_CHUNK), 0)
    sl = lax.broadcasted_iota(I32, (HG_CHUNK, HG_CHUNK), 1)
    mask1 = ((tl // 32) == (sl // 32)) & ((tl % 32) >= 16) & ((sl % 32) < 16)
    rl = lax.broadcasted_iota(I32, (HG_CHUNK, 1), 0)
    hi32 = rl >= 32
    hi16 = (rl % 32) >= 16
    grp1 = rl >= 32

    st = st_ref[...]
    outs = []
    for c in range(rows // HG_CHUNK):
        sl_c = slice(c * HG_CHUNK, (c + 1) * HG_CHUNK)
        qc, kc, vc, bc = q[sl_c], kk[sl_c], v[sl_c], b[sl_c]
        b_last = bc[HG_CHUNK - 1:HG_CHUNK, :]
        r2 = bc[31:32, :]
        r1 = jnp.where(grp1, bc[47:48, :], bc[15:16, :])
        q2 = jnp.where(hi32, qc * jnp.exp(jnp.where(hi32, bc - r2, 0.0)), 0.0)
        k2 = jnp.where(hi32, 0.0, kc * jnp.exp(jnp.where(hi32, 0.0, r2 - bc)))
        q1 = jnp.where(hi16, qc * jnp.exp(jnp.where(hi16, bc - r1, 0.0)), 0.0)
        k1 = jnp.where(hi16, 0.0, kc * jnp.exp(jnp.where(hi16, 0.0, r1 - bc)))
        s2 = lax.dot_general(q2.astype(BF16), k2.astype(BF16), (((1,), (1,)), ((), ())),
                             preferred_element_type=F32)
        s1 = lax.dot_general(q1.astype(BF16), k1.astype(BF16), (((1,), (1,)), ((), ())),
                             preferred_element_type=F32)
        p = s2 + jnp.where(mask1, s1, 0.0)
        o_c = jnp.dot(p.astype(BF16), vc.astype(BF16), preferred_element_type=F32)
        qe = (qc * jnp.exp(bc)).astype(BF16)
        o_c = o_c + lax.dot_general(qe, st.astype(BF16), (((1,), (1,)), ((), ())),
                                    preferred_element_type=F32)
        ke = (kc * jnp.exp(b_last - bc)).astype(BF16)
        upd = lax.dot_general(vc.astype(BF16), ke, (((0,), (0,)), ((), ())),
                              preferred_element_type=F32)
        st = st * jnp.exp(b_last) + upd
        outs.append(o_c)
    st_ref[...] = st
    o = o + jnp.concatenate(outs, axis=0)

    ms = jnp.mean(o * o, axis=-1, keepdims=True)
    oa = o * lax.rsqrt(ms + RMS_EPS) * g_ref[...]
    z = z_ref[...]
    o_ref[...] = (oa * (z * jax.nn.sigmoid(z))).astype(o_ref.dtype)


def _hgrn(a_in, g_in, lb, norm_g, tril16):
    L = a_in.shape[0]
    nh = N_HEADS_A
    blk = lambda off: pl.BlockSpec((HG_ROWS, HEAD_DIM_A), lambda h, s: (s, off + h))
    return pl.pallas_call(
        _hgrn_kernel,
        grid=(nh, L // HG_ROWS),
        in_specs=[pl.BlockSpec((1, HEAD_DIM_A), lambda h, s: (0, h)),
                  pl.BlockSpec((1, HEAD_DIM_A), lambda h, s: (0, h)),
                  blk(0), blk(nh), blk(2 * nh),
                  pl.BlockSpec((HG_ROWS, HEAD_DIM_A), lambda h, s: (s, h)),
                  pl.BlockSpec((HG_ROWS, HG_ROWS), lambda h, s: (0, 0))],
        out_specs=pl.BlockSpec((HG_ROWS, HEAD_DIM_A), lambda h, s: (s, h)),
        out_shape=jax.ShapeDtypeStruct((L, D_A), BF16),
        scratch_shapes=[pltpu.VMEM((HEAD_DIM_A, HEAD_DIM_A), F32)],
        compiler_params=_cparams(("arbitrary", "arbitrary")),
        name="hgrn2",
    )(lb, norm_g, a_in, a_in, a_in, g_in, tril16)


def _bias_table_kernel(rb_ref, o_ref):
    n_c = o_ref.shape[1]
    c = lax.broadcasted_iota(I32, (n_c, TQ), 0)
    t = lax.broadcasted_iota(I32, (n_c, TQ), 1)
    dist = t + KB - c
    max_exact = N_BUCKETS // 2
    d = jnp.maximum(dist, 0)
    df = jnp.maximum(d, 1).astype(F32)
    large = max_exact + (jnp.log(df / max_exact) / math.log(MAX_DISTANCE / max_exact)
                         * (N_BUCKETS - max_exact)).astype(I32)
    large = jnp.minimum(large, N_BUCKETS - 1)
    bucket = jnp.where(d < max_exact, d, large)
    for h in range(N_HEADS_B):
        acc = jnp.zeros((n_c, TQ), F32)
        for k in range(N_BUCKETS):
            acc = jnp.where(bucket == k, rb_ref[k, h], acc)
        o_ref[h] = (acc - rb_ref[N_BUCKETS - 1, h]) * LOG2E


def _bias_table(rel_bias):
    return pl.pallas_call(
        _bias_table_kernel,
        in_specs=[pl.BlockSpec(memory_space=pltpu.SMEM)],
        out_specs=pl.BlockSpec(memory_space=pltpu.VMEM),
        out_shape=jax.ShapeDtypeStruct((N_HEADS_B, NEAR + KB, TQ), F32),
        name="t5_bias_table",
    )(rel_bias)


def _dsa_kernel(qbt_ref, qit_ref, wit_ref, zb_ref, kb_ref, vt_ref, ki_ref, bt_ref, o_ref,
                key_ref, cand_ref, candk_ref, stat_ref, lg_ref, mt_ref, acc_ref, m_ref, l_ref,
                *, topk, idx_bits):
    i = pl.program_id(0)
    t_idx = i * TQ + lax.broadcasted_iota(I32, (1, TQ), 1)
    n_kb = TS // KB
    n_fb = TF // KB
    nt = (TF // TS) * ((i + n_fb) // n_fb)
    row_ts = lax.broadcasted_iota(I32, (TS, 1), 0)

    qit = qit_ref[...]
    w = wit_ref[...] * (IDX_HEADS ** -0.5)
    qi_pairs = [jnp.concatenate([qit[2 * p * IDX_DIM:(2 * p + 1) * IDX_DIM, :],
                                 qit[(2 * p + 1) * IDX_DIM:(2 * p + 2) * IDX_DIM, :]], axis=1)
                for p in range(IDX_HEADS // 2)]

    def score_tile(j, carry):
        kt = ki_ref[pl.ds(j * n_kb, n_kb)].reshape(TS, IDX_DIM)
        acc = jnp.zeros((TS, TQ), F32)
        for p in range(IDX_HEADS // 2):
            dd = jnp.dot(kt, qi_pairs[p], preferred_element_type=F32)
            acc = acc + w[2 * p:2 * p + 1, :] * jnp.maximum(dd[:, :TQ], 0.0)
            acc = acc + w[2 * p + 1:2 * p + 2, :] * jnp.maximum(dd[:, TQ:], 0.0)
        s_idx = j * TS + row_ts
        sc = jnp.where(s_idx <= t_idx, acc, -jnp.inf)
        sc = sc + 0.0
        bits = pltpu.bitcast(sc, I32)
        keys = bits ^ ((bits >> 31) & 0x7FFFFFFF)
        key_ref[pl.ds(j * n_kb, n_kb)] = keys.reshape(n_kb, KB, TQ)
        sc3 = sc.reshape(TS // 8, 8, TQ)
        for g in range(CAND_G):
            lst = [cand_ref[g, k] for k in range(CAND_M)]
            for c in range(TS // 8 // CAND_G):
                x = sc3[g + CAND_G * c]
                for k in range(CAND_M):
                    hi = jnp.maximum(lst[k], x)
                    x = jnp.minimum(lst[k], x)
                    lst[k] = hi
            for k in range(CAND_M):
                cand_ref[g, k] = lst[k]
        return carry

    cand_ref[...] = jnp.full(cand_ref.shape, -jnp.inf, F32)
    lax.fori_loop(0, nt, score_tile, 0)

    def to_keys(sc):
        bits = pltpu.bitcast(sc, I32)
        return bits ^ ((bits >> 31) & 0x7FFFFFFF)

    def count_ge(thr_signed):
        def body(j, cnt):
            keys = key_ref[pl.ds(j * n_kb, n_kb)].reshape(TS // 8, 8, TQ)
            return cnt + _tree_sum([jnp.where(keys[k] >= thr_signed, 1, 0) for k in range(TS // 8)])
        cnt = lax.fori_loop(0, nt, body, jnp.zeros((8, TQ), I32))
        return jnp.sum(cnt, axis=0, keepdims=True)

    def bisect(count_fn):
        def bit_step(bi, tb):
            cand = tb | jnp.left_shift(jnp.int32(1), 31 - bi)
            return jnp.where(count_fn(cand ^ INT_MIN) >= topk, cand, tb)
        tb = lax.fori_loop(0, 32, bit_step, jnp.zeros((1, TQ), I32))
        return jnp.maximum(tb ^ INT_MIN, NEG_INF_KEY + 1)

    stat_ref[2] = jnp.ones((8, TQ), I32)

    @pl.when(nt > CAND_MIN_TILES)
    def _():
        candk_ref[...] = to_keys(cand_ref[...].reshape(CAND_G * CAND_M, 8, TQ))

        def count_cand_ge(thr_signed):
            hits = [jnp.where(candk_ref[k] >= thr_signed, 1, 0) for k in range(CAND_G * CAND_M)]
            return jnp.sum(_tree_sum(hits), axis=0, keepdims=True)

        thr_c = bisect(count_cand_ge)
        stat_ref[0] = jnp.broadcast_to(thr_c, (8, TQ))
        stat_ref[1] = jnp.broadcast_to(count_ge(thr_c), (8, TQ))
        stat_ref[2] = jnp.broadcast_to((count_ge(thr_c + 1) >= topk).astype(I32), (8, TQ))

    @pl.when(jnp.max(stat_ref[2]) > 0)
    def _():
        thr_f = bisect(count_ge)
        stat_ref[0] = jnp.broadcast_to(thr_f, (8, TQ))
        stat_ref[1] = jnp.broadcast_to(count_ge(thr_f), (8, TQ))

    thr = stat_ref[0, 0:1, :]
    cnt_ge_thr = stat_ref[1, 0:1, :]
    excess = (cnt_ge_thr > topk)

    @pl.when(jnp.max(excess.astype(I32)) > 0)
    def _():
        cnt_gt = count_ge(thr + 1)
        need = topk - cnt_gt

        def count_eq_below(cut):
            def body(j, cnt):
                keys = key_ref[pl.ds(j * n_kb, n_kb)].reshape(TS, TQ)
                s_idx = j * TS + row_ts
                hit = (keys == thr) & (s_idx < cut)
                return cnt + jnp.sum(hit.astype(I32).reshape(TS // 8, 8, TQ), axis=0)
            cnt = lax.fori_loop(0, nt, body, jnp.zeros((8, TQ), I32))
            return jnp.sum(cnt, axis=0, keepdims=True)

        def idx_step(bi, cut):
            cand = cut | jnp.left_shift(jnp.int32(1), idx_bits - 1 - bi)
            return jnp.where(count_eq_below(cand) < need, cand, cut)

        cut = lax.fori_loop(0, idx_bits, idx_step, jnp.zeros((1, TQ), I32))
        def demote(j, carry):
            keys = key_ref[pl.ds(j * n_kb, n_kb)].reshape(TS, TQ)
            s_idx = j * TS + row_ts
            drop = excess & (keys == thr) & (s_idx > cut)
            key_ref[pl.ds(j * n_kb, n_kb)] = jnp.where(drop, keys - 1, keys).reshape(n_kb, KB, TQ)
            return carry
        lax.fori_loop(0, nt, demote, 0)

    qbt = qbt_ref[...]
    zq = jnp.zeros((HEAD_DIM_B, TQ), BF16)
    qb_pairs = [jnp.concatenate(
        [jnp.concatenate([qbt[2 * p * HEAD_DIM_B:(2 * p + 1) * HEAD_DIM_B, :], zq], axis=1),
         jnp.concatenate([zq, qbt[(2 * p + 1) * HEAD_DIM_B:(2 * p + 2) * HEAD_DIM_B, :]], axis=1)],
        axis=0) for p in range(N_HEADS_B // 2)]

    m_ref[...] = jnp.full(m_ref.shape, MASK_NEG, F32)
    l_ref[...] = jnp.zeros(l_ref.shape, F32)
    acc_ref[...] = jnp.zeros(acc_ref.shape, F32)

    def stage1(kt, madd, bias_of_head):
        n = kt.shape[0]
        for p in range(N_HEADS_B // 2):
            lg = jnp.dot(kt[:, 2 * p * HEAD_DIM_B:(2 * p + 2) * HEAD_DIM_B], qb_pairs[p],
                         preferred_element_type=F32)
            for hh in range(2):
                h = 2 * p + hh
                lo = lg[:, hh * TQ:(hh + 1) * TQ] + madd
                if bias_of_head is not None:
                    lo = lo + bias_of_head(h)
                lg_ref[h, 0:n, :] = lo
                mt_ref[h] = jnp.broadcast_to(jnp.max(lo, axis=0, keepdims=True), (8, TQ))

    def stage2(vt):
        n = vt.shape[1]
        for h in range(N_HEADS_B):
            m_old = m_ref[h]
            m_new = jnp.maximum(m_old, mt_ref[h])
            alpha = jnp.exp2(m_old - m_new)
            m_ref[h] = m_new
            pe = jnp.exp2(lg_ref[h, 0:n, :] - m_new[0:1, :]).astype(BF16)
            rs = slice(h * HEAD_DIM_B, (h + 1) * HEAD_DIM_B)
            vt_aug = jnp.concatenate([vt[rs, :], jnp.ones((16, n), BF16)], axis=0)
            pv = jnp.dot(vt_aug, pe, preferred_element_type=F32)
            l_ref[h] = alpha * l_ref[h] + pv[HEAD_DIM_B:HEAD_DIM_B + 8, :]
            acc_ref[rs, :] = alpha[0:1, :] * acc_ref[rs, :] + pv[0:HEAD_DIM_B, :]

    far_end = (i - 1) * KB
    nf = (i + n_fb - 2) // n_fb
    row_tf = lax.broadcasted_iota(I32, (TF, 1), 0)

    def far_tile(j, carry):
        kt = kb_ref[pl.ds(j * n_fb, n_fb)].reshape(TF, D_B)
        keys = key_ref[pl.ds(j * n_fb, n_fb)].reshape(TF, TQ)
        s_idx = j * TF + row_tf
        madd = jnp.where((keys >= thr) & (s_idx < far_end), 0.0, MASK_NEG)
        stage1(kt, madd, None)
        stage2(jnp.concatenate([vt_ref[j * n_fb + c] for c in range(n_fb)], axis=1))
        return carry
    lax.fori_loop(0, nf, far_tile, 0)

    nb0 = jnp.maximum(i - 1, 0)
    toff = jnp.where(i == 0, KB, 0)
    kt = kb_ref[pl.ds(nb0, NEAR // KB)].reshape(NEAR, D_B)
    keys = key_ref[pl.ds(nb0, NEAR // KB)].reshape(NEAR, TQ)
    madd = jnp.where(keys >= thr, 0.0, MASK_NEG)
    stage1(kt, madd, lambda h: bt_ref[h, pl.ds(pl.multiple_of(toff, KB), NEAR), :])
    stage2(jnp.concatenate([vt_ref[nb0 + c] for c in range(NEAR // KB)], axis=1))

    ot = jnp.concatenate([acc_ref[h * HEAD_DIM_B:(h + 1) * HEAD_DIM_B, :] * (1.0 / l_ref[h, 0:1, :])
                          for h in range(N_HEADS_B)], axis=0)
    ob = ot.T
    z = zb_ref[...]
    o_ref[...] = (ob * (z * jax.nn.sigmoid(z))).astype(o_ref.dtype)


def _dsa(qbt, qit, wit, g_in, kb3, vt3, ki3, btab, topk):
    L = qbt.shape[1]
    n_blk = L // KB
    idx_bits = max(1, (L - 1).bit_length())
    whole = pl.BlockSpec(memory_space=pltpu.VMEM)
    return pl.pallas_call(
        functools.partial(_dsa_kernel, topk=topk, idx_bits=idx_bits),
        grid=(L // TQ,),
        in_specs=[pl.BlockSpec((D_B, TQ), lambda i: (0, i)),
                  pl.BlockSpec((IDX_HEADS * IDX_DIM, TQ), lambda i: (0, i)),
                  pl.BlockSpec((16, TQ), lambda i: (0, i)),
                  pl.BlockSpec((TQ, D_B), lambda i: (i, 1)),
                  whole, whole, whole, whole],
        out_specs=pl.BlockSpec((TQ, D_B), lambda i: (i, 0)),
        out_shape=jax.ShapeDtypeStruct((L, D_B), BF16),
        scratch_shapes=[pltpu.VMEM((n_blk, KB, TQ), I32),
                        pltpu.VMEM((CAND_G, CAND_M, 8, TQ), F32),
                        pltpu.VMEM((CAND_G * CAND_M, 8, TQ), I32),
                        pltpu.VMEM((3, 8, TQ), I32),
                        pltpu.VMEM((N_HEADS_B, TF, TQ), F32),
                        pltpu.VMEM((N_HEADS_B, 8, TQ), F32),
                        pltpu.VMEM((D_B, TQ), F32),
                        pltpu.VMEM((N_HEADS_B, 8, TQ), F32),
                        pltpu.VMEM((N_HEADS_B, 8, TQ), F32)],
        compiler_params=_cparams(("arbitrary",)),
        name="dsa",
    )(qbt, qit, wit, g_in, kb3, vt3, ki3, btab)


def _out_kernel(x_ref, ga_ref, gb_ref, a_ref, bq_ref, wa_ref, wb_ref, wo_ref, lg_ref, lb_ref,
                o_ref, o16_ref, *, alpha):
    ua = jnp.dot(a_ref[...], wa_ref[...], preferred_element_type=F32)
    ub = jnp.dot(bq_ref[...], wb_ref[...], preferred_element_type=F32)
    merged = jax.nn.sigmoid(ga_ref[...]) * ua + jax.nn.sigmoid(gb_ref[...]) * ub
    y = jnp.dot(merged.astype(BF16), wo_ref[...], preferred_element_type=F32)
    r = alpha * x_ref[...] + y
    mu = jnp.mean(r, axis=-1, keepdims=True)
    var = jnp.mean(jnp.square(r - mu), axis=-1, keepdims=True)
    out = (r - mu) * lax.rsqrt(var + LN_EPS) * lg_ref[...] + lb_ref[...]
    o_ref[...] = out
    o16_ref[...] = out.astype(BF16)


def _out_block(x, g_in, ga_in, gb_in, wa16, wb16, wo16, ln_g, ln_b, alpha, tm):
    L = x.shape[0]
    const = lambda shape: pl.BlockSpec(shape, lambda i: (0, 0))
    return pl.pallas_call(
        functools.partial(_out_kernel, alpha=alpha),
        grid=(L // tm,),
        in_specs=[pl.BlockSpec((tm, D_MODEL), lambda i: (i, 0)),
                  pl.BlockSpec((tm, D_MODEL), lambda i: (i, 1)),
                  pl.BlockSpec((tm, D_MODEL), lambda i: (i, 2)),
                  pl.BlockSpec((tm, D_A), lambda i: (i, 0)),
                  pl.BlockSpec((tm, D_B), lambda i: (i, 0)),
                  const((D_A, D_MODEL)), const((D_B, D_MODEL)), const((D_MODEL, D_MODEL)),
                  const((1, D_MODEL)), const((1, D_MODEL))],
        out_specs=[pl.BlockSpec((tm, D_MODEL), lambda i: (i, 0)),
                   pl.BlockSpec((tm, D_MODEL), lambda i: (i, 0))],
        out_shape=[jax.ShapeDtypeStruct((L, D_MODEL), F32),
                   jax.ShapeDtypeStruct((L, D_MODEL), BF16)],
        compiler_params=_cparams(("arbitrary",)),
        name="out_block",
    )(x, g_in, g_in, ga_in, gb_in, wa16, wb16, wo16, ln_g, ln_b)


def _layer(h, h16, w_in, b_in, w_up_a, w_up_b, w_out, lb, norm_g, btab, ln_g, ln_b, tril16,
           alpha, topk):
    L = h.shape[0]
    tm = min(512, L)
    c0 = 4 * D_A
    col = lambda a, n: (w_in[:, a:a + n], b_in[a:a + n])

    def normal(a, n, dtype, tn, scale=1.0):
        w, b = col(a, n)
        return _proj(h16, (w * scale).astype(BF16), (b * scale)[None, :], dtype, tm, tn)

    def transposed(a, n, dtype, blocked, scale=1.0, pad_to=None):
        w, b = col(a, n)
        wt = (w * scale).T
        bc = (b * scale)[:, None]
        if pad_to is not None:
            wt = jnp.pad(wt, ((0, pad_to - n), (0, 0)))
            bc = jnp.pad(bc, ((0, pad_to - n), (0, 0)))
        return _proj_t(wt.astype(BF16), h16, bc, dtype, KB if blocked else tm, blocked)

    a_in = normal(0, 3 * D_A, F32, 512)
    gparts = [col(3 * D_A, D_A), col(c0 + 3 * D_B, D_B), col(N_IN - 2 * D_MODEL, 2 * D_MODEL)]
    g_in = _proj(h16, jnp.concatenate([w for w, _ in gparts], axis=1).astype(BF16),
                 jnp.concatenate([b for _, b in gparts])[None, :], F32, tm, 512)
    qbt = transposed(c0, D_B, BF16, False, scale=HEAD_DIM_B ** -0.5 * LOG2E)
    kb3 = normal(c0 + D_B, D_B, BF16, 512).reshape(L // KB, KB, D_B)
    vt3 = transposed(c0 + 2 * D_B, D_B, BF16, True)
    ci = c0 + 4 * D_B
    qit = transposed(ci, IDX_HEADS * IDX_DIM, BF16, False, scale=IDX_DIM ** -0.5)
    ki3 = normal(ci + IDX_HEADS * IDX_DIM, IDX_DIM, BF16, IDX_DIM).reshape(L // KB, KB, IDX_DIM)
    wit = transposed(ci + IDX_HEADS * IDX_DIM + IDX_DIM, IDX_HEADS, F32, False, pad_to=16)

    ga_in = _hgrn(a_in, g_in, lb[None, :], norm_g[None, :], tril16)
    gb_in = _dsa(qbt, qit, wit, g_in, kb3, vt3, ki3, btab, topk)
    return _out_block(h, g_in, ga_in, gb_in, w_up_a.astype(BF16), w_up_b.astype(BF16),
                      w_out.astype(BF16), ln_g[None, :], ln_b[None, :], alpha, tm)


N_IN = 4 * D_A + 4 * D_B + IDX_HEADS * IDX_DIM + IDX_DIM + IDX_HEADS + 2 * D_MODEL


def kernel(x, w_in, b_in, w_up_a, w_up_b, w_out, lb_logits, norm_a_g, rel_bias, ln_g, ln_b):
    depth = w_in.shape[0]
    batch, L, _ = x.shape
    alpha = (2 * depth) ** 0.25
    topk = min(TOPK_MAX, L // 4)
    lbs = jnp.cumsum(jax.nn.softmax(lb_logits.astype(F32), axis=0), axis=0)
    lbs = lbs - lbs[0:1]
    r = jnp.arange(HG_ROWS)
    tril16 = ((r[:, None] >= r[None, :]) & ((r[:, None] // HG_CHUNK) == (r[None, :] // HG_CHUNK))).astype(BF16)
    btab = _bias_table(rel_bias.astype(F32))
    outs = []
    for bi in range(batch):
        h = x[bi]
        h16 = h.astype(BF16)
        for layer in range(depth):
            h, h16 = _layer(h, h16, w_in[layer], b_in[layer], w_up_a[layer], w_up_b[layer], w_out[layer],
                            lbs[layer], norm_a_g[layer], btab, ln_g[layer], ln_b[layer], tril16, alpha, topk)
        outs.append(h)
    return jnp.stack(outs, axis=0)
```

```python
import functools
import math

import jax
import jax.numpy as jnp
from jax import lax
from jax.experimental import pallas as pl
from jax.experimental.pallas import tpu as pltpu

F32 = jnp.float32
BF16 = jnp.bfloat16
I32 = jnp.int32

D_MODEL = 1024
D_A = 512
HEAD_DIM_A = 128
N_HEADS_A = D_A // HEAD_DIM_A
N_HEADS_B = 8
HEAD_DIM_B = 64
D_B = N_HEADS_B * HEAD_DIM_B
IDX_HEADS = 8
IDX_DIM = 64
TOPK_MAX = 256
N_BUCKETS = 32
MAX_DISTANCE = 128
LN_EPS = 1e-5
RMS_EPS = 1e-6

LANES = 128
VMEM_LIMIT_BYTES = 58 * 1024 * 1024

HG_ROWS = 512
HG_CHUNK = 64
HG_SUB = 16

TQ = 128
KB = 128
TS = 1024
TF = 1024
NEAR = 256
MASK_NEG = -1e30
CAND_G = 16
CAND_M = 12
CAND_MIN_TILES = 2
LOG2E = 1.4426950408889634

INT_MIN = -2147483648
NEG_INF_KEY = -2139095041


def _cparams(sem):
    return pltpu.CompilerParams(dimension_semantics=sem, vmem_limit_bytes=VMEM_LIMIT_BYTES)


def _tree_sum(xs):
    xs = list(xs)
    while len(xs) > 1:
        nxt = [xs[a] + xs[a + 1] for a in range(0, len(xs) - 1, 2)]
        if len(xs) % 2:
            nxt.append(xs[-1])
        xs = nxt
    return xs[0]


PN_A = 3 * D_A
PN_G = D_A + D_B + 2 * D_MODEL
PN = PN_A + PN_G + D_B + IDX_DIM
PT_W = 16
PT = D_B + IDX_HEADS * IDX_DIM + D_B + PT_W
PROJ_CHUNK = 512


def _proj_kernel(x_ref, wn_ref, bn_ref, wt_ref, bt_ref,
                 a_ref, g_ref, kb_ref, ki_ref, qbt_ref, qit_ref, vt_ref, wit_ref):
    x = x_ref[...]
    tm = x.shape[0]

    def rows(c0, n):
        return jnp.dot(x, wn_ref[:, c0:c0 + n], preferred_element_type=F32) + bn_ref[:, c0:c0 + n]

    def cols(r0, n):
        return lax.dot_general(wt_ref[r0:r0 + n, :], x, (((1,), (1,)), ((), ())),
                               preferred_element_type=F32) + bt_ref[r0:r0 + n, :]

    for c in range(PN_A // PROJ_CHUNK):
        a_ref[:, c * PROJ_CHUNK:(c + 1) * PROJ_CHUNK] = rows(c * PROJ_CHUNK, PROJ_CHUNK)
    for c in range(PN_G // PROJ_CHUNK):
        g_ref[:, c * PROJ_CHUNK:(c + 1) * PROJ_CHUNK] = rows(PN_A + c * PROJ_CHUNK, PROJ_CHUNK).astype(BF16)
    kb_ref[...] = rows(PN_A + PN_G, D_B).astype(BF16)
    ki_ref[...] = rows(PN_A + PN_G + D_B, IDX_DIM).astype(BF16)
    qbt_ref[...] = cols(0, D_B).astype(BF16)
    qit_ref[...] = cols(D_B, IDX_HEADS * IDX_DIM).astype(BF16)
    vt = cols(D_B + IDX_HEADS * IDX_DIM, D_B).astype(BF16)
    for c in range(tm // KB):
        vt_ref[c] = vt[:, c * KB:(c + 1) * KB]
    wit_ref[...] = cols(2 * D_B + IDX_HEADS * IDX_DIM, PT_W)


def _proj(x16, wn16, bn, wt16, bt, tm):
    L, k = x16.shape
    row = lambda n: pl.BlockSpec((tm, n), lambda i: (i, 0))
    colT = lambda n: pl.BlockSpec((n, tm), lambda i: (0, i))
    const = lambda shape: pl.BlockSpec(shape, lambda i: (0, 0))
    sds = jax.ShapeDtypeStruct
    return pl.pallas_call(
        _proj_kernel,
        grid=(L // tm,),
        in_specs=[row(k), const((k, PN)), const((1, PN)), const((PT, k)), const((PT, 1))],
        out_specs=[row(PN_A), row(PN_G), row(D_B), row(IDX_DIM),
                   colT(D_B), colT(IDX_HEADS * IDX_DIM),
                   pl.BlockSpec((tm // KB, D_B, KB), lambda i: (i, 0, 0)), colT(PT_W)],
        out_shape=[sds((L, PN_A), F32), sds((L, PN_G), BF16), sds((L, D_B), BF16), sds((L, IDX_DIM), BF16),
                   sds((D_B, L), BF16), sds((IDX_HEADS * IDX_DIM, L), BF16),
                   sds((L // KB, D_B, KB), BF16), sds((PT_W, L), F32)],
        compiler_params=_cparams(("arbitrary",)),
        name="proj",
    )(x16, wn16, bn, wt16, bt)


def _split3_dot(t16, x):
    x1 = x.astype(BF16)
    r1 = x - x1.astype(F32)
    x2 = r1.astype(BF16)
    r2 = r1 - x2.astype(F32)
    x3 = r2.astype(BF16)
    return (jnp.dot(t16, x1, preferred_element_type=F32)
            + jnp.dot(t16, x2, preferred_element_type=F32)
            + jnp.dot(t16, x3, preferred_element_type=F32))


def _hgrn_kernel(lb_ref, g_ref, q_ref, f_ref, i_ref, z_ref, tril_ref, o_ref, st_ref):
    @pl.when(pl.program_id(1) == 0)
    def _():
        st_ref[...] = jnp.zeros_like(st_ref)

    rows = q_ref.shape[0]
    q = q_ref[...]
    fl = f_ref[...]
    v = i_ref[...]
    lb = lb_ref[...]
    log_lb = jnp.log(lb)
    log_1m = jnp.log(1.0 - lb)
    log_sig = jnp.minimum(fl, 0.0) - jnp.log(1.0 + jnp.exp(-jnp.abs(fl)))
    bb = log_1m + log_sig
    mx = jnp.maximum(log_lb, bb)
    log_f = mx + jnp.log(1.0 + jnp.exp(-jnp.abs(log_lb - bb)))
    kk = (1.0 - lb) * jax.nn.sigmoid(-fl)
    b = _split3_dot(tril_ref[...], log_f)

    row_l = lax.broadcasted_iota(I32, (rows, 1), 0) % HG_SUB
    ones16 = jnp.ones((LANES, LANES), BF16)
    qk0 = (q * kk).astype(BF16)
    o = jnp.dot(qk0, ones16, preferred_element_type=F32) * v
    for d in range(1, HG_SUB):
        valid = (row_l + d) < HG_SUB
        qd = pltpu.roll(q, rows - d, 0)
        bd = pltpu.roll(b, rows - d, 0)
        dec = jnp.exp(jnp.where(valid, bd - b, 0.0))
        e = jnp.where(valid, qd * kk * dec, 0.0).astype(BF16)
        c = jnp.dot(e, ones16, preferred_element_type=F32) * v
        o = o + pltpu.roll(c, d, 0)

    tl = lax.broadcasted_iota(I32, (HG_CHUNK, HG_CHUNK), 0)
    sl = lax.broadcasted_iota(I32, (HG_CHUNK, HG_CHUNK), 1)
    mask1 = ((tl // 32) == (sl // 32)) & ((tl % 32) >= 16) & ((sl % 32) < 16)
    rl = lax.broadcasted_iota(I32, (HG_CHUNK, 1), 0)
    hi32 = rl >= 32
    hi16 = (rl % 32) >= 16
    grp1 = rl >= 32

    st = st_ref[...]
    outs = []
    for c in range(rows // HG_CHUNK):
        sl_c = slice(c * HG_CHUNK, (c + 1) * HG_CHUNK)
        qc, kc, vc, bc = q[sl_c], kk[sl_c], v[sl_c], b[sl_c]
        b_last = bc[HG_CHUNK - 1:HG_CHUNK, :]
        r2 = bc[31:32, :]
        r1 = jnp.where(grp1, bc[47:48, :], bc[15:16, :])
        q2 = jnp.where(hi32, qc * jnp.exp(jnp.where(hi32, bc - r2, 0.0)), 0.0)
        k2 = jnp.where(hi32, 0.0, kc * jnp.exp(jnp.where(hi32, 0.0, r2 - bc)))
        q1 = jnp.where(hi16, qc * jnp.exp(jnp.where(hi16, bc - r1, 0.0)), 0.0)
        k1 = jnp.where(hi16, 0.0, kc * jnp.exp(jnp.where(hi16, 0.0, r1 - bc)))
        s2 = lax.dot_general(q2.astype(BF16), k2.astype(BF16), (((1,), (1,)), ((), ())),
                             preferred_element_type=F32)
        s1 = lax.dot_general(q1.astype(BF16), k1.astype(BF16), (((1,), (1,)), ((), ())),
                             preferred_element_type=F32)
        p = s2 + jnp.where(mask1, s1, 0.0)
        o_c = jnp.dot(p.astype(BF16), vc.astype(BF16), preferred_element_type=F32)
        qe = (qc * jnp.exp(bc)).astype(BF16)
        o_c = o_c + lax.dot_general(qe, st.astype(BF16), (((1,), (1,)), ((), ())),
                                    preferred_element_type=F32)
        ke = (kc * jnp.exp(b_last - bc)).astype(BF16)
        upd = lax.dot_general(vc.astype(BF16), ke, (((0,), (0,)), ((), ())),
                              preferred_element_type=F32)
        st = st * jnp.exp(b_last) + upd
        outs.append(o_c)
    st_ref[...] = st
    o = o + jnp.concatenate(outs, axis=0)

    ms = jnp.mean(o * o, axis=-1, keepdims=True)
    oa = o * lax.rsqrt(ms + RMS_EPS) * g_ref[...]
    z = z_ref[...].astype(F32)
    o_ref[...] = (oa * (z * jax.nn.sigmoid(z))).astype(o_ref.dtype)


def _hgrn(a_in, g_in, lb, norm_g, tril16):
    L = a_in.shape[0]
    nh = N_HEADS_A
    blk = lambda off: pl.BlockSpec((HG_ROWS, HEAD_DIM_A), lambda h, s: (s, off + h))
    return pl.pallas_call(
        _hgrn_kernel,
        grid=(nh, L // HG_ROWS),
        in_specs=[pl.BlockSpec((1, HEAD_DIM_A), lambda h, s: (0, h)),
                  pl.BlockSpec((1, HEAD_DIM_A), lambda h, s: (0, h)),
                  blk(0), blk(nh), blk(2 * nh),
                  pl.BlockSpec((HG_ROWS, HEAD_DIM_A), lambda h, s: (s, h)),
                  pl.BlockSpec((HG_ROWS, HG_ROWS), lambda h, s: (0, 0))],
        out_specs=pl.BlockSpec((HG_ROWS, HEAD_DIM_A), lambda h, s: (s, h)),
        out_shape=jax.ShapeDtypeStruct((L, D_A), BF16),
        scratch_shapes=[pltpu.VMEM((HEAD_DIM_A, HEAD_DIM_A), F32)],
        compiler_params=_cparams(("arbitrary", "arbitrary")),
        name="hgrn2",
    )(lb, norm_g, a_in, a_in, a_in, g_in, tril16)


def _bias_table_kernel(rb_ref, o_ref):
    n_c = o_ref.shape[1]
    c = lax.broadcasted_iota(I32, (n_c, TQ), 0)
    t = lax.broadcasted_iota(I32, (n_c, TQ), 1)
    dist = t + KB - c
    max_exact = N_BUCKETS // 2
    d = jnp.maximum(dist, 0)
    df = jnp.maximum(d, 1).astype(F32)
    large = max_exact + (jnp.log(df / max_exact) / math.log(MAX_DISTANCE / max_exact)
                         * (N_BUCKETS - max_exact)).astype(I32)
    large = jnp.minimum(large, N_BUCKETS - 1)
    bucket = jnp.where(d < max_exact, d, large)
    for h in range(N_HEADS_B):
        acc = jnp.zeros((n_c, TQ), F32)
        for k in range(N_BUCKETS):
            acc = jnp.where(bucket == k, rb_ref[k, h], acc)
        o_ref[h] = (acc - rb_ref[N_BUCKETS - 1, h]) * LOG2E


def _bias_table(rel_bias):
    return pl.pallas_call(
        _bias_table_kernel,
        in_specs=[pl.BlockSpec(memory_space=pltpu.SMEM)],
        out_specs=pl.BlockSpec(memory_space=pltpu.VMEM),
        out_shape=jax.ShapeDtypeStruct((N_HEADS_B, NEAR + KB, TQ), F32),
        name="t5_bias_table",
    )(rel_bias)


def _dsa_kernel(qbt_ref, qit_ref, wit_ref, zb_ref, kb_ref, vt_ref, ki_ref, bt_ref, o_ref,
                key_ref, cand_ref, candk_ref, stat_ref, lg_ref, mt_ref, acc_ref, m_ref, l_ref,
                *, topk, idx_bits):
    i = pl.program_id(0)
    t_idx = i * TQ + lax.broadcasted_iota(I32, (1, TQ), 1)
    n_kb = TS // KB
    n_fb = TF // KB
    nt = (TF // TS) * ((i + n_fb) // n_fb)
    row_ts = lax.broadcasted_iota(I32, (TS, 1), 0)

    qit = qit_ref[...]
    w = wit_ref[...] * (IDX_HEADS ** -0.5)
    qi_pairs = [jnp.concatenate([qit[2 * p * IDX_DIM:(2 * p + 1) * IDX_DIM, :],
                                 qit[(2 * p + 1) * IDX_DIM:(2 * p + 2) * IDX_DIM, :]], axis=1)
                for p in range(IDX_HEADS // 2)]

    def score_tile(j, carry):
        kt = ki_ref[pl.ds(j * n_kb, n_kb)].reshape(TS, IDX_DIM)
        acc = jnp.zeros((TS, TQ), F32)
        for p in range(IDX_HEADS // 2):
            dd = jnp.dot(kt, qi_pairs[p], preferred_element_type=F32)
            acc = acc + w[2 * p:2 * p + 1, :] * jnp.maximum(dd[:, :TQ], 0.0)
            acc = acc + w[2 * p + 1:2 * p + 2, :] * jnp.maximum(dd[:, TQ:], 0.0)
        s_idx = j * TS + row_ts
        sc = jnp.where(s_idx <= t_idx, acc, -jnp.inf)
        sc = sc + 0.0
        bits = pltpu.bitcast(sc, I32)
        keys = bits ^ ((bits >> 31) & 0x7FFFFFFF)
        key_ref[pl.ds(j * n_kb, n_kb)] = keys.reshape(n_kb, KB, TQ)
        sc3 = sc.reshape(TS // 8, 8, TQ)
        for g in range(CAND_G):
            lst = [cand_ref[g, k] for k in range(CAND_M)]
            for c in range(TS // 8 // CAND_G):
                x = sc3[g + CAND_G * c]
                for k in range(CAND_M):
                    hi = jnp.maximum(lst[k], x)
                    x = jnp.minimum(lst[k], x)
                    lst[k] = hi
            for k in range(CAND_M):
                cand_ref[g, k] = lst[k]
        return carry

    cand_ref[...] = jnp.full(cand_ref.shape, -jnp.inf, F32)
    lax.fori_loop(0, nt, score_tile, 0)

    def to_keys(sc):
        bits = pltpu.bitcast(sc, I32)
        return bits ^ ((bits >> 31) & 0x7FFFFFFF)

    def count_ge(thr_signed):
        def body(j, cnt):
            keys = key_ref[pl.ds(j * n_kb, n_kb)].reshape(TS // 8, 8, TQ)
            return cnt + _tree_sum([jnp.where(keys[k] >= thr_signed, 1, 0) for k in range(TS // 8)])
        cnt = lax.fori_loop(0, nt, body, jnp.zeros((8, TQ), I32))
        return jnp.sum(cnt, axis=0, keepdims=True)

    def bisect(count_fn):
        def bit_step(bi, tb):
            cand = tb | jnp.left_shift(jnp.int32(1), 31 - bi)
            return jnp.where(count_fn(cand ^ INT_MIN) >= topk, cand, tb)
        tb = lax.fori_loop(0, 32, bit_step, jnp.zeros((1, TQ), I32))
        return jnp.maximum(tb ^ INT_MIN, NEG_INF_KEY + 1)

    stat_ref[2] = jnp.ones((8, TQ), I32)

    @pl.when(nt > CAND_MIN_TILES)
    def _():
        candk_ref[...] = to_keys(cand_ref[...].reshape(CAND_G * CAND_M, 8, TQ))

        def count_cand_ge(thr_signed):
            hits = [jnp.where(candk_ref[k] >= thr_signed, 1, 0) for k in range(CAND_G * CAND_M)]
            return jnp.sum(_tree_sum(hits), axis=0, keepdims=True)

        thr_c = bisect(count_cand_ge)
        stat_ref[0] = jnp.broadcast_to(thr_c, (8, TQ))
        stat_ref[1] = jnp.broadcast_to(count_ge(thr_c), (8, TQ))
        stat_ref[2] = jnp.broadcast_to((count_ge(thr_c + 1) >= topk).astype(I32), (8, TQ))

    @pl.when(jnp.max(stat_ref[2]) > 0)
    def _():
        thr_f = bisect(count_ge)
        stat_ref[0] = jnp.broadcast_to(thr_f, (8, TQ))
        stat_ref[1] = jnp.broadcast_to(count_ge(thr_f), (8, TQ))

    thr = stat_ref[0, 0:1, :]
    cnt_ge_thr = stat_ref[1, 0:1, :]
    excess = (cnt_ge_thr > topk)

    @pl.when(jnp.max(excess.astype(I32)) > 0)
    def _():
        cnt_gt = count_ge(thr + 1)
        need = topk - cnt_gt

        def count_eq_below(cut):
            def body(j, cnt):
                keys = key_ref[pl.ds(j * n_kb, n_kb)].reshape(TS, TQ)
                s_idx = j * TS + row_ts
                hit = (keys == thr) & (s_idx < cut)
                return cnt + jnp.sum(hit.astype(I32).reshape(TS // 8, 8, TQ), axis=0)
            cnt = lax.fori_loop(0, nt, body, jnp.zeros((8, TQ), I32))
            return jnp.sum(cnt, axis=0, keepdims=True)

        def idx_step(bi, cut):
            cand = cut | jnp.left_shift(jnp.int32(1), idx_bits - 1 - bi)
            return jnp.where(count_eq_below(cand) < need, cand, cut)

        cut = lax.fori_loop(0, idx_bits, idx_step, jnp.zeros((1, TQ), I32))
        def demote(j, carry):
            keys = key_ref[pl.ds(j * n_kb, n_kb)].reshape(TS, TQ)
            s_idx = j * TS + row_ts
            drop = excess & (keys == thr) & (s_idx > cut)
            key_ref[pl.ds(j * n_kb, n_kb)] = jnp.where(drop, keys - 1, keys).reshape(n_kb, KB, TQ)
            return carry
        lax.fori_loop(0, nt, demote, 0)

    qbt = qbt_ref[...]
    zq = jnp.zeros((HEAD_DIM_B, TQ), BF16)
    qb_pairs = [jnp.concatenate(
        [jnp.concatenate([qbt[2 * p * HEAD_DIM_B:(2 * p + 1) * HEAD_DIM_B, :], zq], axis=1),
         jnp.concatenate([zq, qbt[(2 * p + 1) * HEAD_DIM_B:(2 * p + 2) * HEAD_DIM_B, :]], axis=1)],
        axis=0) for p in range(N_HEADS_B // 2)]

    m_ref[...] = jnp.full(m_ref.shape, MASK_NEG, F32)
    l_ref[...] = jnp.zeros(l_ref.shape, F32)
    acc_ref[...] = jnp.zeros(acc_ref.shape, F32)

    def stage1(kt, madd, bias_of_head):
        n = kt.shape[0]
        for p in range(N_HEADS_B // 2):
            lg = jnp.dot(kt[:, 2 * p * HEAD_DIM_B:(2 * p + 2) * HEAD_DIM_B], qb_pairs[p],
                         preferred_element_type=F32)
            for hh in range(2):
                h = 2 * p + hh
                lo = lg[:, hh * TQ:(hh + 1) * TQ] + madd
                if bias_of_head is not None:
                    lo = lo + bias_of_head(h)
                lg_ref[h, 0:n, :] = lo
                mt_ref[h] = jnp.broadcast_to(jnp.max(lo, axis=0, keepdims=True), (8, TQ))

    def stage2(vt):
        n = vt.shape[1]
        for h in range(N_HEADS_B):
            m_old = m_ref[h]
            m_new = jnp.maximum(m_old, mt_ref[h])
            alpha = jnp.exp2(m_old - m_new)
            m_ref[h] = m_new
            pe = jnp.exp2(lg_ref[h, 0:n, :] - m_new[0:1, :]).astype(BF16)
            rs = slice(h * HEAD_DIM_B, (h + 1) * HEAD_DIM_B)
            vt_aug = jnp.concatenate([vt[rs, :], jnp.ones((16, n), BF16)], axis=0)
            pv = jnp.dot(vt_aug, pe, preferred_element_type=F32)
            l_ref[h] = alpha * l_ref[h] + pv[HEAD_DIM_B:HEAD_DIM_B + 8, :]
            acc_ref[rs, :] = alpha[0:1, :] * acc_ref[rs, :] + pv[0:HEAD_DIM_B, :]

    far_end = (i - 1) * KB
    nf = (i + n_fb - 2) // n_fb
    row_tf = lax.broadcasted_iota(I32, (TF, 1), 0)

    def far_tile(j, carry):
        kt = kb_ref[pl.ds(j * n_fb, n_fb)].reshape(TF, D_B)
        keys = key_ref[pl.ds(j * n_fb, n_fb)].reshape(TF, TQ)
        s_idx = j * TF + row_tf
        madd = jnp.where((keys >= thr) & (s_idx < far_end), 0.0, MASK_NEG)
        stage1(kt, madd, None)
        stage2(jnp.concatenate([vt_ref[j * n_fb + c] for c in range(n_fb)], axis=1))
        return carry
    lax.fori_loop(0, nf, far_tile, 0)

    nb0 = jnp.maximum(i - 1, 0)
    toff = jnp.where(i == 0, KB, 0)
    kt = kb_ref[pl.ds(nb0, NEAR // KB)].reshape(NEAR, D_B)
    keys = key_ref[pl.ds(nb0, NEAR // KB)].reshape(NEAR, TQ)
    madd = jnp.where(keys >= thr, 0.0, MASK_NEG)
    stage1(kt, madd, lambda h: bt_ref[h, pl.ds(pl.multiple_of(toff, KB), NEAR), :])
    stage2(jnp.concatenate([vt_ref[nb0 + c] for c in range(NEAR // KB)], axis=1))

    ot = jnp.concatenate([acc_ref[h * HEAD_DIM_B:(h + 1) * HEAD_DIM_B, :] * (1.0 / l_ref[h, 0:1, :])
                          for h in range(N_HEADS_B)], axis=0)
    ob = ot.T
    z = zb_ref[...].astype(F32)
    o_ref[...] = (ob * (z * jax.nn.sigmoid(z))).astype(o_ref.dtype)


def _dsa(qbt, qit, wit, g_in, kb3, vt3, ki3, btab, topk):
    L = qbt.shape[1]
    n_blk = L // KB
    idx_bits = max(1, (L - 1).bit_length())
    whole = pl.BlockSpec(memory_space=pltpu.VMEM)
    return pl.pallas_call(
        functools.partial(_dsa_kernel, topk=topk, idx_bits=idx_bits),
        grid=(L // TQ,),
        in_specs=[pl.BlockSpec((D_B, TQ), lambda i: (0, i)),
                  pl.BlockSpec((IDX_HEADS * IDX_DIM, TQ), lambda i: (0, i)),
                  pl.BlockSpec((16, TQ), lambda i: (0, i)),
                  pl.BlockSpec((TQ, D_B), lambda i: (i, 1)),
                  whole, whole, whole, whole],
        out_specs=pl.BlockSpec((TQ, D_B), lambda i: (i, 0)),
        out_shape=jax.ShapeDtypeStruct((L, D_B), BF16),
        scratch_shapes=[pltpu.VMEM((n_blk, KB, TQ), I32),
                        pltpu.VMEM((CAND_G, CAND_M, 8, TQ), F32),
                        pltpu.VMEM((CAND_G * CAND_M, 8, TQ), I32),
                        pltpu.VMEM((3, 8, TQ), I32),
                        pltpu.VMEM((N_HEADS_B, TF, TQ), F32),
                        pltpu.VMEM((N_HEADS_B, 8, TQ), F32),
                        pltpu.VMEM((D_B, TQ), F32),
                        pltpu.VMEM((N_HEADS_B, 8, TQ), F32),
                        pltpu.VMEM((N_HEADS_B, 8, TQ), F32)],
        compiler_params=_cparams(("arbitrary",)),
        name="dsa",
    )(qbt, qit, wit, g_in, kb3, vt3, ki3, btab)


def _out_kernel(x_ref, ga_ref, gb_ref, a_ref, bq_ref, wa_ref, wb_ref, wo_ref, lg_ref, lb_ref,
                o_ref, o16_ref, *, alpha):
    ua = jnp.dot(a_ref[...], wa_ref[...], preferred_element_type=F32)
    ub = jnp.dot(bq_ref[...], wb_ref[...], preferred_element_type=F32)
    merged = (jax.nn.sigmoid(ga_ref[...].astype(F32)) * ua
              + jax.nn.sigmoid(gb_ref[...].astype(F32)) * ub)
    y = jnp.dot(merged.astype(BF16), wo_ref[...], preferred_element_type=F32)
    r = alpha * x_ref[...] + y
    mu = jnp.mean(r, axis=-1, keepdims=True)
    var = jnp.mean(jnp.square(r - mu), axis=-1, keepdims=True)
    out = (r - mu) * lax.rsqrt(var + LN_EPS) * lg_ref[...] + lb_ref[...]
    o_ref[...] = out
    o16_ref[...] = out.astype(BF16)


def _out_block(x, g_in, ga_in, gb_in, wa16, wb16, wo16, ln_g, ln_b, alpha, tm):
    L = x.shape[0]
    const = lambda shape: pl.BlockSpec(shape, lambda i: (0, 0))
    return pl.pallas_call(
        functools.partial(_out_kernel, alpha=alpha),
        grid=(L // tm,),
        in_specs=[pl.BlockSpec((tm, D_MODEL), lambda i: (i, 0)),
                  pl.BlockSpec((tm, D_MODEL), lambda i: (i, 1)),
                  pl.BlockSpec((tm, D_MODEL), lambda i: (i, 2)),
                  pl.BlockSpec((tm, D_A), lambda i: (i, 0)),
                  pl.BlockSpec((tm, D_B), lambda i: (i, 0)),
                  const((D_A, D_MODEL)), const((D_B, D_MODEL)), const((D_MODEL, D_MODEL)),
                  const((1, D_MODEL)), const((1, D_MODEL))],
        out_specs=[pl.BlockSpec((tm, D_MODEL), lambda i: (i, 0)),
                   pl.BlockSpec((tm, D_MODEL), lambda i: (i, 0))],
        out_shape=[jax.ShapeDtypeStruct((L, D_MODEL), F32),
                   jax.ShapeDtypeStruct((L, D_MODEL), BF16)],
        compiler_params=_cparams(("arbitrary",)),
        name="out_block",
    )(x, g_in, g_in, ga_in, gb_in, wa16, wb16, wo16, ln_g, ln_b)


def _prep_in_proj(w_in, b_in):
    c0 = 4 * D_A
    ci = c0 + 4 * D_B
    cw = ci + IDX_HEADS * IDX_DIM + IDX_DIM
    cg = cw + IDX_HEADS
    row_cols = [(0, 4 * D_A), (c0 + 3 * D_B, D_B), (cg, 2 * D_MODEL), (c0 + D_B, D_B),
                (ci + IDX_HEADS * IDX_DIM, IDX_DIM)]
    wn = jnp.concatenate([w_in[:, :, a:a + n] for a, n in row_cols], axis=2).astype(BF16)
    bn = jnp.concatenate([b_in[:, a:a + n] for a, n in row_cols], axis=1)[:, None, :]
    t_cols = [(c0, D_B, HEAD_DIM_B ** -0.5 * LOG2E), (ci, IDX_HEADS * IDX_DIM, IDX_DIM ** -0.5),
              (c0 + 2 * D_B, D_B, 1.0), (cw, IDX_HEADS, 1.0)]
    pad = PT_W - IDX_HEADS
    wt = jnp.concatenate([w_in[:, :, a:a + n] * s for a, n, s in t_cols], axis=2)
    wt = jnp.pad(wt, ((0, 0), (0, 0), (0, pad))).transpose(0, 2, 1).astype(BF16)
    bt = jnp.concatenate([b_in[:, a:a + n] * s for a, n, s in t_cols], axis=1)
    bt = jnp.pad(bt, ((0, 0), (0, pad)))[:, :, None]
    return wn, bn, wt, bt


def _layer(h, h16, wn, bn, wt, bt, wa16, wb16, wo16, lb, norm_g, btab, ln_g, ln_b, tril16, alpha, topk):
    L = h.shape[0]
    tm = min(512, L)
    a_in, g_in, kb, ki, qbt, qit, vt3, wit = _proj(h16, wn, bn, wt, bt, tm)
    ga_in = _hgrn(a_in, g_in, lb[None, :], norm_g[None, :], tril16)
    gb_in = _dsa(qbt, qit, wit, g_in, kb.reshape(L // KB, KB, D_B), vt3,
                 ki.reshape(L // KB, KB, IDX_DIM), btab, topk)
    return _out_block(h, g_in, ga_in, gb_in, wa16, wb16, wo16, ln_g[None, :], ln_b[None, :], alpha, tm)


def kernel(x, w_in, b_in, w_up_a, w_up_b, w_out, lb_logits, norm_a_g, rel_bias, ln_g, ln_b):
    depth = w_in.shape[0]
    batch, L, _ = x.shape
    alpha = (2 * depth) ** 0.25
    topk = min(TOPK_MAX, L // 4)
    lbs = jnp.cumsum(jax.nn.softmax(lb_logits.astype(F32), axis=0), axis=0)
    lbs = lbs - lbs[0:1]
    r = jnp.arange(HG_ROWS)
    tril16 = ((r[:, None] >= r[None, :]) & ((r[:, None] // HG_CHUNK) == (r[None, :] // HG_CHUNK))).astype(BF16)
    btab = _bias_table(rel_bias.astype(F32))
    wn, bn, wt, bt = _prep_in_proj(w_in, b_in)
    wa16, wb16, wo16 = w_up_a.astype(BF16), w_up_b.astype(BF16), w_out.astype(BF16)
    outs = []
    for bi in range(batch):
        h = x[bi]
        h16 = h.astype(BF16)
        for layer in range(depth):
            h, h16 = _layer(h, h16, wn[layer], bn[layer], wt[layer], bt[layer],
                            wa16[layer], wb16[layer], wo16[layer],
                            lbs[layer], norm_a_g[layer], btab, ln_g[layer], ln_b[layer], tril16, alpha, topk)
        outs.append(h)
    return jnp.stack(outs, axis=0)
```

```python
import functools
import math

import jax
import jax.numpy as jnp
from jax import lax
from jax.experimental import pallas as pl
from jax.experimental.pallas import tpu as pltpu

F32 = jnp.float32
BF16 = jnp.bfloat16
I32 = jnp.int32

D_MODEL = 1024
D_A = 512
HEAD_DIM_A = 128
N_HEADS_A = D_A // HEAD_DIM_A
N_HEADS_B = 8
HEAD_DIM_B = 64
D_B = N_HEADS_B * HEAD_DIM_B
IDX_HEADS = 8
IDX_DIM = 64
TOPK_MAX = 256
N_BUCKETS = 32
MAX_DISTANCE = 128
LN_EPS = 1e-5
RMS_EPS = 1e-6

LANES = 128
VMEM_LIMIT_BYTES = 58 * 1024 * 1024

HG_ROWS = 512
HG_CHUNK = 64
HG_SUB = 8

TQ = 128
KB = 128
TS = 1024
TF = 1024
NEAR = 256
MASK_NEG = -1e30
CAND_G = 16
CAND_M = 12
CAND_MIN_TILES = 2
LOG2E = 1.4426950408889634

INT_MIN = -2147483648
NEG_INF_KEY = -2139095041


def _cparams(sem):
    return pltpu.CompilerParams(dimension_semantics=sem, vmem_limit_bytes=VMEM_LIMIT_BYTES)


def _tree_sum(xs):
    xs = list(xs)
    while len(xs) > 1:
        nxt = [xs[a] + xs[a + 1] for a in range(0, len(xs) - 1, 2)]
        if len(xs) % 2:
            nxt.append(xs[-1])
        xs = nxt
    return xs[0]


def _oddeven_merge(lo, hi, r):
    step = r * 2
    if step < hi - lo:
        yield from _oddeven_merge(lo, hi, step)
        yield from _oddeven_merge(lo + r, hi, step)
        yield from [(a, a + r) for a in range(lo + r, hi - r, step)]
    else:
        yield (lo, lo + r)


def _oddeven_sort(lo, hi):
    if hi - lo >= 1:
        mid = lo + (hi - lo) // 2
        yield from _oddeven_sort(lo, mid)
        yield from _oddeven_sort(mid + 1, hi)
        yield from _oddeven_merge(lo, hi, 1)


def _sorted_desc(xs):
    xs = list(xs)
    for a, b in _oddeven_sort(0, len(xs) - 1):
        xs[a], xs[b] = jnp.maximum(xs[a], xs[b]), jnp.minimum(xs[a], xs[b])
    return xs


@functools.lru_cache(maxsize=None)
def _merge_top_plan(m, nb):
    n = 1
    while n < 2 * max(m, nb):
        n *= 2
    pos = [None] * n
    for k in range(m):
        pos[k] = ("l", k)
    for k in range(nb):
        pos[n // 2 + k] = ("b", k)
    ops = []
    for a, b in _oddeven_merge(0, n - 1, 1):
        va, vb = pos[a], pos[b]
        if vb is None:
            continue
        if va is None:
            pos[a], pos[b] = vb, None
            continue
        mx, mn = ("t", len(ops), 0), ("t", len(ops), 1)
        ops.append((mx, mn, va, vb))
        pos[a], pos[b] = mx, mn
    outs = tuple(pos[:m])
    live = set(outs)
    kept = []
    for mx, mn, va, vb in reversed(ops):
        if mx in live or mn in live:
            kept.append((mx if mx in live else None, mn if mn in live else None, va, vb))
            live.update((va, vb))
    return tuple(reversed(kept)), outs


def _merge_top(lst, batch):
    ops, outs = _merge_top_plan(len(lst), len(batch))
    env = {("l", k): x for k, x in enumerate(lst)}
    env.update({("b", k): x for k, x in enumerate(batch)})
    for mx, mn, va, vb in ops:
        if mx is not None:
            env[mx] = jnp.maximum(env[va], env[vb])
        if mn is not None:
            env[mn] = jnp.minimum(env[va], env[vb])
    return [env[o] for o in outs]


PN_A = 3 * D_A
PN_G = D_A + D_B + 2 * D_MODEL
PN = PN_A + PN_G + D_B + IDX_DIM
PT_W = 16
PT = D_B + IDX_HEADS * IDX_DIM + D_B + PT_W
PROJ_CHUNK = 512


def _proj_kernel(x_ref, wn_ref, bn_ref, wt_ref, bt_ref,
                 a_ref, g_ref, kb_ref, ki_ref, qbt_ref, qit_ref, vt_ref, wit_ref):
    x = x_ref[...]
    tm = x.shape[0]

    def rows(c0, n):
        return jnp.dot(x, wn_ref[:, c0:c0 + n], preferred_element_type=F32) + bn_ref[:, c0:c0 + n]

    def cols(r0, n):
        return lax.dot_general(wt_ref[r0:r0 + n, :], x, (((1,), (1,)), ((), ())),
                               preferred_element_type=F32) + bt_ref[r0:r0 + n, :]

    for c in range(PN_A // PROJ_CHUNK):
        a_ref[:, c * PROJ_CHUNK:(c + 1) * PROJ_CHUNK] = rows(c * PROJ_CHUNK, PROJ_CHUNK)
    for c in range(PN_G // PROJ_CHUNK):
        g_ref[:, c * PROJ_CHUNK:(c + 1) * PROJ_CHUNK] = rows(PN_A + c * PROJ_CHUNK, PROJ_CHUNK).astype(BF16)
    kb_ref[...] = rows(PN_A + PN_G, D_B).astype(BF16)
    ki_ref[...] = rows(PN_A + PN_G + D_B, IDX_DIM).astype(BF16)
    qbt_ref[...] = cols(0, D_B).astype(BF16)
    qit_ref[...] = cols(D_B, IDX_HEADS * IDX_DIM).astype(BF16)
    vt = cols(D_B + IDX_HEADS * IDX_DIM, D_B).astype(BF16)
    for c in range(tm // KB):
        vt_ref[c] = vt[:, c * KB:(c + 1) * KB]
    wit_ref[...] = cols(2 * D_B + IDX_HEADS * IDX_DIM, PT_W)


def _proj(x16, wn16, bn, wt16, bt, tm):
    L, k = x16.shape
    row = lambda n: pl.BlockSpec((tm, n), lambda i: (i, 0))
    colT = lambda n: pl.BlockSpec((n, tm), lambda i: (0, i))
    const = lambda shape: pl.BlockSpec(shape, lambda i: (0, 0))
    sds = jax.ShapeDtypeStruct
    return pl.pallas_call(
        _proj_kernel,
        grid=(L // tm,),
        in_specs=[row(k), const((k, PN)), const((1, PN)), const((PT, k)), const((PT, 1))],
        out_specs=[row(PN_A), row(PN_G), row(D_B), row(IDX_DIM),
                   colT(D_B), colT(IDX_HEADS * IDX_DIM),
                   pl.BlockSpec((tm // KB, D_B, KB), lambda i: (i, 0, 0)), colT(PT_W)],
        out_shape=[sds((L, PN_A), F32), sds((L, PN_G), BF16), sds((L, D_B), BF16), sds((L, IDX_DIM), BF16),
                   sds((D_B, L), BF16), sds((IDX_HEADS * IDX_DIM, L), BF16),
                   sds((L // KB, D_B, KB), BF16), sds((PT_W, L), F32)],
        compiler_params=_cparams(("arbitrary",)),
        name="proj",
    )(x16, wn16, bn, wt16, bt)


def _split3_dot(t16, x):
    x1 = x.astype(BF16)
    r1 = x - x1.astype(F32)
    x2 = r1.astype(BF16)
    r2 = r1 - x2.astype(F32)
    x3 = r2.astype(BF16)
    return (jnp.dot(t16, x1, preferred_element_type=F32)
            + jnp.dot(t16, x2, preferred_element_type=F32)
            + jnp.dot(t16, x3, preferred_element_type=F32))


def _hgrn_kernel(lb_ref, g_ref, q_ref, f_ref, i_ref, z_ref, tril_ref, o_ref, st_ref):
    @pl.when(pl.program_id(1) == 0)
    def _():
        st_ref[...] = jnp.zeros_like(st_ref)

    rows = q_ref.shape[0]
    q = q_ref[...]
    fl = f_ref[...]
    v = i_ref[...]
    lb = lb_ref[...]
    log_lb = jnp.log(lb)
    log_1m = jnp.log(1.0 - lb)
    log_sig = jnp.minimum(fl, 0.0) - jnp.log(1.0 + jnp.exp(-jnp.abs(fl)))
    bb = log_1m + log_sig
    mx = jnp.maximum(log_lb, bb)
    log_f = mx + jnp.log(1.0 + jnp.exp(-jnp.abs(log_lb - bb)))
    kk = (1.0 - lb) * jax.nn.sigmoid(-fl)
    b = _split3_dot(tril_ref[...], log_f * LOG2E)

    row_l = lax.broadcasted_iota(I32, (rows, 1), 0) % HG_SUB
    ones16 = jnp.ones((LANES, LANES), BF16)
    qk0 = (q * kk).astype(BF16)
    o = jnp.dot(qk0, ones16, preferred_element_type=F32) * v
    for d in range(1, HG_SUB):
        valid = (row_l + d) < HG_SUB
        qd = pltpu.roll(q, rows - d, 0)
        bd = pltpu.roll(b, rows - d, 0)
        dec = jnp.exp2(jnp.where(valid, bd - b, 0.0))
        e = jnp.where(valid, qd * kk * dec, 0.0).astype(BF16)
        c = jnp.dot(e, ones16, preferred_element_type=F32) * v
        o = o + pltpu.roll(c, d, 0)

    tl = lax.broadcasted_iota(I32, (HG_CHUNK, HG_CHUNK), 0)
    sl = lax.broadcasted_iota(I32, (HG_CHUNK, HG_CHUNK), 1)
    rl = lax.broadcasted_iota(I32, (HG_CHUNK, 1), 0)
    widths = []
    w = HG_SUB
    while w < HG_CHUNK:
        widths.append(w)
        w *= 2
    pair_mask = {w: ((tl // (2 * w)) == (sl // (2 * w))) & ((tl % (2 * w)) >= w) & ((sl % (2 * w)) < w)
                 for w in widths}
    is_query = {w: (rl % (2 * w)) >= w for w in widths}

    def pair_scores(w, qc, kc, bc):
        hi = is_query[w]
        ref = bc[w - 1:w, :]
        for g in range(1, HG_CHUNK // (2 * w)):
            ref = jnp.where((rl // (2 * w)) == g, bc[g * 2 * w + w - 1:g * 2 * w + w, :], ref)
        ql = jnp.where(hi, qc * jnp.exp2(jnp.where(hi, bc - ref, 0.0)), 0.0)
        kl = jnp.where(hi, 0.0, kc * jnp.exp2(jnp.where(hi, 0.0, ref - bc)))
        s = lax.dot_general(ql.astype(BF16), kl.astype(BF16), (((1,), (1,)), ((), ())),
                            preferred_element_type=F32)
        return jnp.where(pair_mask[w], s, 0.0)

    st = st_ref[...]
    outs = []
    for c in range(rows // HG_CHUNK):
        sl_c = slice(c * HG_CHUNK, (c + 1) * HG_CHUNK)
        qc, kc, vc, bc = q[sl_c], kk[sl_c], v[sl_c], b[sl_c]
        b_last = bc[HG_CHUNK - 1:HG_CHUNK, :]
        p = pair_scores(widths[0], qc, kc, bc)
        for w in widths[1:]:
            p = p + pair_scores(w, qc, kc, bc)
        o_c = jnp.dot(p.astype(BF16), vc.astype(BF16), preferred_element_type=F32)
        qe = (qc * jnp.exp2(bc)).astype(BF16)
        o_c = o_c + lax.dot_general(qe, st.astype(BF16), (((1,), (1,)), ((), ())),
                                    preferred_element_type=F32)
        ke = (kc * jnp.exp2(b_last - bc)).astype(BF16)
        upd = lax.dot_general(vc.astype(BF16), ke, (((0,), (0,)), ((), ())),
                              preferred_element_type=F32)
        st = st * jnp.exp2(b_last) + upd
        outs.append(o_c)
    st_ref[...] = st
    o = o + jnp.concatenate(outs, axis=0)

    ms = jnp.mean(o * o, axis=-1, keepdims=True)
    oa = o * lax.rsqrt(ms + RMS_EPS) * g_ref[...]
    z = z_ref[...].astype(F32)
    o_ref[...] = (oa * (z * jax.nn.sigmoid(z))).astype(o_ref.dtype)


def _hgrn(a_in, g_in, lb, norm_g, tril16):
    L = a_in.shape[0]
    nh = N_HEADS_A
    blk = lambda off: pl.BlockSpec((HG_ROWS, HEAD_DIM_A), lambda h, s: (s, off + h))
    return pl.pallas_call(
        _hgrn_kernel,
        grid=(nh, L // HG_ROWS),
        in_specs=[pl.BlockSpec((1, HEAD_DIM_A), lambda h, s: (0, h)),
                  pl.BlockSpec((1, HEAD_DIM_A), lambda h, s: (0, h)),
                  blk(0), blk(nh), blk(2 * nh),
                  pl.BlockSpec((HG_ROWS, HEAD_DIM_A), lambda h, s: (s, h)),
                  pl.BlockSpec((HG_ROWS, HG_ROWS), lambda h, s: (0, 0))],
        out_specs=pl.BlockSpec((HG_ROWS, HEAD_DIM_A), lambda h, s: (s, h)),
        out_shape=jax.ShapeDtypeStruct((L, D_A), BF16),
        scratch_shapes=[pltpu.VMEM((HEAD_DIM_A, HEAD_DIM_A), F32)],
        compiler_params=_cparams(("arbitrary", "arbitrary")),
        name="hgrn2",
    )(lb, norm_g, a_in, a_in, a_in, g_in, tril16)


def _bias_table_kernel(rb_ref, o_ref):
    n_c = o_ref.shape[1]
    c = lax.broadcasted_iota(I32, (n_c, TQ), 0)
    t = lax.broadcasted_iota(I32, (n_c, TQ), 1)
    dist = t + KB - c
    max_exact = N_BUCKETS // 2
    d = jnp.maximum(dist, 0)
    df = jnp.maximum(d, 1).astype(F32)
    large = max_exact + (jnp.log(df / max_exact) / math.log(MAX_DISTANCE / max_exact)
                         * (N_BUCKETS - max_exact)).astype(I32)
    large = jnp.minimum(large, N_BUCKETS - 1)
    bucket = jnp.where(d < max_exact, d, large)
    for h in range(N_HEADS_B):
        acc = jnp.zeros((n_c, TQ), F32)
        for k in range(N_BUCKETS):
            acc = jnp.where(bucket == k, rb_ref[k, h], acc)
        o_ref[h] = (acc - rb_ref[N_BUCKETS - 1, h]) * LOG2E


def _bias_table(rel_bias):
    return pl.pallas_call(
        _bias_table_kernel,
        in_specs=[pl.BlockSpec(memory_space=pltpu.SMEM)],
        out_specs=pl.BlockSpec(memory_space=pltpu.VMEM),
        out_shape=jax.ShapeDtypeStruct((N_HEADS_B, NEAR + KB, TQ), F32),
        name="t5_bias_table",
    )(rel_bias)


def _dsa_kernel(qbt_ref, qit_ref, wit_ref, zb_ref, kb_ref, vt_ref, ki_ref, bt_ref, o_ref,
                key_ref, cand_ref, candk_ref, stat_ref, lg_ref, mt_ref, acc_ref, m_ref, l_ref,
                *, topk, idx_bits):
    i = pl.program_id(0)
    t_idx = i * TQ + lax.broadcasted_iota(I32, (1, TQ), 1)
    n_kb = TS // KB
    n_fb = TF // KB
    nt = (TF // TS) * ((i + n_fb) // n_fb)
    row_ts = lax.broadcasted_iota(I32, (TS, 1), 0)

    qit = qit_ref[...]
    w = wit_ref[...] * (IDX_HEADS ** -0.5)
    qi_pairs = [jnp.concatenate([qit[2 * p * IDX_DIM:(2 * p + 1) * IDX_DIM, :],
                                 qit[(2 * p + 1) * IDX_DIM:(2 * p + 2) * IDX_DIM, :]], axis=1)
                for p in range(IDX_HEADS // 2)]

    def score_tile(j, carry):
        kt = ki_ref[pl.ds(j * n_kb, n_kb)].reshape(TS, IDX_DIM)
        acc = jnp.zeros((TS, TQ), F32)
        for p in range(IDX_HEADS // 2):
            dd = jnp.dot(kt, qi_pairs[p], preferred_element_type=F32)
            acc = acc + w[2 * p:2 * p + 1, :] * jnp.maximum(dd[:, :TQ], 0.0)
            acc = acc + w[2 * p + 1:2 * p + 2, :] * jnp.maximum(dd[:, TQ:], 0.0)
        s_idx = j * TS + row_ts
        sc = jnp.where(s_idx <= t_idx, acc, -jnp.inf)
        sc = sc + 0.0
        bits = pltpu.bitcast(sc, I32)
        keys = bits ^ ((bits >> 31) & 0x7FFFFFFF)
        key_ref[pl.ds(j * n_kb, n_kb)] = keys.reshape(n_kb, KB, TQ)
        sc3 = sc.reshape(TS // 8, 8, TQ)
        for g in range(CAND_G):
            lst = [cand_ref[g, k] for k in range(CAND_M)]
            batch = _sorted_desc([sc3[g + CAND_G * c] for c in range(TS // 8 // CAND_G)])
            lst = _merge_top(lst, batch)
            for k in range(CAND_M):
                cand_ref[g, k] = lst[k]
        return carry

    cand_ref[...] = jnp.full(cand_ref.shape, -jnp.inf, F32)
    lax.fori_loop(0, nt, score_tile, 0)

    def to_keys(sc):
        bits = pltpu.bitcast(sc, I32)
        return bits ^ ((bits >> 31) & 0x7FFFFFFF)

    def count_ge(thr_signed):
        def body(j, cnt):
            keys = key_ref[pl.ds(j * n_kb, n_kb)].reshape(TS // 8, 8, TQ)
            return cnt + _tree_sum([jnp.where(keys[k] >= thr_signed, 1, 0) for k in range(TS // 8)])
        cnt = lax.fori_loop(0, nt, body, jnp.zeros((8, TQ), I32))
        return jnp.sum(cnt, axis=0, keepdims=True)

    def bisect(count_fn, prefix=None, nbits=32):
        def bit_step(bi, tb):
            cand = tb | jnp.left_shift(jnp.int32(1), nbits - 1 - bi)
            return jnp.where(count_fn(cand ^ INT_MIN) >= topk, cand, tb)
        tb = lax.fori_loop(0, nbits, bit_step, jnp.zeros((1, TQ), I32) if prefix is None else prefix)
        return jnp.maximum(tb ^ INT_MIN, NEG_INF_KEY + 1)

    stat_ref[2] = jnp.ones((8, TQ), I32)

    @pl.when(nt > CAND_MIN_TILES)
    def _():
        candk_ref[...] = to_keys(cand_ref[...].reshape(CAND_G * CAND_M, 8, TQ))

        def count_cand_ge(thr_signed):
            hits = [jnp.where(candk_ref[k] >= thr_signed, 1, 0) for k in range(CAND_G * CAND_M)]
            return jnp.sum(_tree_sum(hits), axis=0, keepdims=True)

        n_cls = 8 * CAND_G
        rank = -(-topk // n_cls)
        if rank <= CAND_M:
            rows_r = [candk_ref[g * CAND_M + rank - 1] for g in range(CAND_G)]
            hi_b = jnp.max(functools.reduce(jnp.maximum, rows_r), axis=0, keepdims=True) ^ INT_MIN
            lo_b = jnp.min(functools.reduce(jnp.minimum, rows_r), axis=0, keepdims=True) ^ INT_MIN
            diff = hi_b ^ lo_b
            expo = (pltpu.bitcast(diff.astype(F32), I32) >> 23) & 0xFF
            bits = jnp.where(diff < 0, 32, jnp.where(diff == 0, 0, expo - 126))
            nbits = jnp.minimum(jnp.max(bits), 32)
            low = jnp.where(nbits >= 32, jnp.int32(-1),
                            jnp.left_shift(jnp.int32(1), jnp.minimum(nbits, 31)) - 1)
            thr_c = bisect(count_cand_ge, hi_b & ~low, nbits)
        else:
            thr_c = bisect(count_cand_ge)
        stat_ref[0] = jnp.broadcast_to(thr_c, (8, TQ))
        stat_ref[1] = jnp.broadcast_to(count_ge(thr_c), (8, TQ))
        stat_ref[2] = jnp.broadcast_to((count_ge(thr_c + 1) >= topk).astype(I32), (8, TQ))

    @pl.when(jnp.max(stat_ref[2]) > 0)
    def _():
        thr_f = bisect(count_ge)
        stat_ref[0] = jnp.broadcast_to(thr_f, (8, TQ))
        stat_ref[1] = jnp.broadcast_to(count_ge(thr_f), (8, TQ))

    thr = stat_ref[0, 0:1, :]
    cnt_ge_thr = stat_ref[1, 0:1, :]
    excess = (cnt_ge_thr > topk)

    @pl.when(jnp.max(excess.astype(I32)) > 0)
    def _():
        cnt_gt = count_ge(thr + 1)
        need = topk - cnt_gt

        def count_eq_below(cut):
            def body(j, cnt):
                keys = key_ref[pl.ds(j * n_kb, n_kb)].reshape(TS, TQ)
                s_idx = j * TS + row_ts
                hit = (keys == thr) & (s_idx < cut)
                return cnt + jnp.sum(hit.astype(I32).reshape(TS // 8, 8, TQ), axis=0)
            cnt = lax.fori_loop(0, nt, body, jnp.zeros((8, TQ), I32))
            return jnp.sum(cnt, axis=0, keepdims=True)

        def idx_step(bi, cut):
            cand = cut | jnp.left_shift(jnp.int32(1), idx_bits - 1 - bi)
            return jnp.where(count_eq_below(cand) < need, cand, cut)

        cut = lax.fori_loop(0, idx_bits, idx_step, jnp.zeros((1, TQ), I32))
        def demote(j, carry):
            keys = key_ref[pl.ds(j * n_kb, n_kb)].reshape(TS, TQ)
            s_idx = j * TS + row_ts
            drop = excess & (keys == thr) & (s_idx > cut)
            key_ref[pl.ds(j * n_kb, n_kb)] = jnp.where(drop, keys - 1, keys).reshape(n_kb, KB, TQ)
            return carry
        lax.fori_loop(0, nt, demote, 0)

    qbt = qbt_ref[...]
    zq = jnp.zeros((HEAD_DIM_B, TQ), BF16)
    qb_pairs = [jnp.concatenate(
        [jnp.concatenate([qbt[2 * p * HEAD_DIM_B:(2 * p + 1) * HEAD_DIM_B, :], zq], axis=1),
         jnp.concatenate([zq, qbt[(2 * p + 1) * HEAD_DIM_B:(2 * p + 2) * HEAD_DIM_B, :]], axis=1)],
        axis=0) for p in range(N_HEADS_B // 2)]

    m_ref[...] = jnp.full(m_ref.shape, MASK_NEG, F32)
    l_ref[...] = jnp.zeros(l_ref.shape, F32)
    acc_ref[...] = jnp.zeros(acc_ref.shape, F32)

    def stage1(kt, madd, bias_of_head):
        n = kt.shape[0]
        for p in range(N_HEADS_B // 2):
            lg = jnp.dot(kt[:, 2 * p * HEAD_DIM_B:(2 * p + 2) * HEAD_DIM_B], qb_pairs[p],
                         preferred_element_type=F32)
            for hh in range(2):
                h = 2 * p + hh
                lo = lg[:, hh * TQ:(hh + 1) * TQ] + madd
                if bias_of_head is not None:
                    lo = lo + bias_of_head(h)
                lg_ref[h, 0:n, :] = lo
                mt_ref[h] = jnp.broadcast_to(jnp.max(lo, axis=0, keepdims=True), (8, TQ))

    def stage2(vt):
        n = vt.shape[1]
        for h in range(N_HEADS_B):
            m_old = m_ref[h]
            m_new = jnp.maximum(m_old, mt_ref[h])
            alpha = jnp.exp2(m_old - m_new)
            m_ref[h] = m_new
            pe = jnp.exp2(lg_ref[h, 0:n, :] - m_new[0:1, :]).astype(BF16)
            rs = slice(h * HEAD_DIM_B, (h + 1) * HEAD_DIM_B)
            vt_aug = jnp.concatenate([vt[rs, :], jnp.ones((16, n), BF16)], axis=0)
            pv = jnp.dot(vt_aug, pe, preferred_element_type=F32)
            l_ref[h] = alpha * l_ref[h] + pv[HEAD_DIM_B:HEAD_DIM_B + 8, :]
            acc_ref[rs, :] = alpha[0:1, :] * acc_ref[rs, :] + pv[0:HEAD_DIM_B, :]

    far_end = (i - 1) * KB
    nf = (i + n_fb - 2) // n_fb
    row_tf = lax.broadcasted_iota(I32, (TF, 1), 0)

    def far_tile(j, carry):
        kt = kb_ref[pl.ds(j * n_fb, n_fb)].reshape(TF, D_B)
        keys = key_ref[pl.ds(j * n_fb, n_fb)].reshape(TF, TQ)
        s_idx = j * TF + row_tf
        madd = jnp.where((keys >= thr) & (s_idx < far_end), 0.0, MASK_NEG)
        stage1(kt, madd, None)
        stage2(jnp.concatenate([vt_ref[j * n_fb + c] for c in range(n_fb)], axis=1))
        return carry
    lax.fori_loop(0, nf, far_tile, 0)

    nb0 = jnp.maximum(i - 1, 0)
    toff = jnp.where(i == 0, KB, 0)
    kt = kb_ref[pl.ds(nb0, NEAR // KB)].reshape(NEAR, D_B)
    keys = key_ref[pl.ds(nb0, NEAR // KB)].reshape(NEAR, TQ)
    madd = jnp.where(keys >= thr, 0.0, MASK_NEG)
    stage1(kt, madd, lambda h: bt_ref[h, pl.ds(pl.multiple_of(toff, KB), NEAR), :])
    stage2(jnp.concatenate([vt_ref[nb0 + c] for c in range(NEAR // KB)], axis=1))

    ot = jnp.concatenate([acc_ref[h * HEAD_DIM_B:(h + 1) * HEAD_DIM_B, :] * (1.0 / l_ref[h, 0:1, :])
                          for h in range(N_HEADS_B)], axis=0)
    ob = ot.T
    z = zb_ref[...].astype(F32)
    o_ref[...] = (ob * (z * jax.nn.sigmoid(z))).astype(o_ref.dtype)


def _dsa(qbt, qit, wit, g_in, kb3, vt3, ki3, btab, topk):
    L = qbt.shape[1]
    n_blk = L // KB
    idx_bits = max(1, (L - 1).bit_length())
    whole = pl.BlockSpec(memory_space=pltpu.VMEM)
    return pl.pallas_call(
        functools.partial(_dsa_kernel, topk=topk, idx_bits=idx_bits),
        grid=(L // TQ,),
        in_specs=[pl.BlockSpec((D_B, TQ), lambda i: (0, i)),
                  pl.BlockSpec((IDX_HEADS * IDX_DIM, TQ), lambda i: (0, i)),
                  pl.BlockSpec((16, TQ), lambda i: (0, i)),
                  pl.BlockSpec((TQ, D_B), lambda i: (i, 1)),
                  whole, whole, whole, whole],
        out_specs=pl.BlockSpec((TQ, D_B), lambda i: (i, 0)),
        out_shape=jax.ShapeDtypeStruct((L, D_B), BF16),
        scratch_shapes=[pltpu.VMEM((n_blk, KB, TQ), I32),
                        pltpu.VMEM((CAND_G, CAND_M, 8, TQ), F32),
                        pltpu.VMEM((CAND_G * CAND_M, 8, TQ), I32),
                        pltpu.VMEM((3, 8, TQ), I32),
                        pltpu.VMEM((N_HEADS_B, TF, TQ), F32),
                        pltpu.VMEM((N_HEADS_B, 8, TQ), F32),
                        pltpu.VMEM((D_B, TQ), F32),
                        pltpu.VMEM((N_HEADS_B, 8, TQ), F32),
                        pltpu.VMEM((N_HEADS_B, 8, TQ), F32)],
        compiler_params=_cparams(("arbitrary",)),
        name="dsa",
    )(qbt, qit, wit, g_in, kb3, vt3, ki3, btab)


def _out_kernel(x_ref, ga_ref, gb_ref, a_ref, bq_ref, wa_ref, wb_ref, wo_ref, lg_ref, lb_ref,
                o_ref, o16_ref, *, alpha):
    ua = jnp.dot(a_ref[...], wa_ref[...], preferred_element_type=F32)
    ub = jnp.dot(bq_ref[...], wb_ref[...], preferred_element_type=F32)
    merged = (jax.nn.sigmoid(ga_ref[...].astype(F32)) * ua
              + jax.nn.sigmoid(gb_ref[...].astype(F32)) * ub)
    y = jnp.dot(merged.astype(BF16), wo_ref[...], preferred_element_type=F32)
    r = alpha * x_ref[...] + y
    mu = jnp.mean(r, axis=-1, keepdims=True)
    var = jnp.mean(jnp.square(r - mu), axis=-1, keepdims=True)
    out = (r - mu) * lax.rsqrt(var + LN_EPS) * lg_ref[...] + lb_ref[...]
    o_ref[...] = out
    o16_ref[...] = out.astype(BF16)


def _out_block(x, g_in, ga_in, gb_in, wa16, wb16, wo16, ln_g, ln_b, alpha, tm):
    L = x.shape[0]
    const = lambda shape: pl.BlockSpec(shape, lambda i: (0, 0))
    return pl.pallas_call(
        functools.partial(_out_kernel, alpha=alpha),
        grid=(L // tm,),
        in_specs=[pl.BlockSpec((tm, D_MODEL), lambda i: (i, 0)),
                  pl.BlockSpec((tm, D_MODEL), lambda i: (i, 1)),
                  pl.BlockSpec((tm, D_MODEL), lambda i: (i, 2)),
                  pl.BlockSpec((tm, D_A), lambda i: (i, 0)),
                  pl.BlockSpec((tm, D_B), lambda i: (i, 0)),
                  const((D_A, D_MODEL)), const((D_B, D_MODEL)), const((D_MODEL, D_MODEL)),
                  const((1, D_MODEL)), const((1, D_MODEL))],
        out_specs=[pl.BlockSpec((tm, D_MODEL), lambda i: (i, 0)),
                   pl.BlockSpec((tm, D_MODEL), lambda i: (i, 0))],
        out_shape=[jax.ShapeDtypeStruct((L, D_MODEL), F32),
                   jax.ShapeDtypeStruct((L, D_MODEL), BF16)],
        compiler_params=_cparams(("arbitrary",)),
        name="out_block",
    )(x, g_in, g_in, ga_in, gb_in, wa16, wb16, wo16, ln_g, ln_b)


def _prep_in_proj(w_in, b_in):
    c0 = 4 * D_A
    ci = c0 + 4 * D_B
    cw = ci + IDX_HEADS * IDX_DIM + IDX_DIM
    cg = cw + IDX_HEADS
    row_cols = [(0, 4 * D_A), (c0 + 3 * D_B, D_B), (cg, 2 * D_MODEL), (c0 + D_B, D_B),
                (ci + IDX_HEADS * IDX_DIM, IDX_DIM)]
    wn = jnp.concatenate([w_in[:, :, a:a + n] for a, n in row_cols], axis=2).astype(BF16)
    bn = jnp.concatenate([b_in[:, a:a + n] for a, n in row_cols], axis=1)[:, None, :]
    t_cols = [(c0, D_B, HEAD_DIM_B ** -0.5 * LOG2E), (ci, IDX_HEADS * IDX_DIM, IDX_DIM ** -0.5),
              (c0 + 2 * D_B, D_B, 1.0), (cw, IDX_HEADS, 1.0)]
    pad = PT_W - IDX_HEADS
    wt = jnp.concatenate([w_in[:, :, a:a + n] * s for a, n, s in t_cols], axis=2)
    wt = jnp.pad(wt, ((0, 0), (0, 0), (0, pad))).transpose(0, 2, 1).astype(BF16)
    bt = jnp.concatenate([b_in[:, a:a + n] * s for a, n, s in t_cols], axis=1)
    bt = jnp.pad(bt, ((0, 0), (0, pad)))[:, :, None]
    return wn, bn, wt, bt


def _layer(h, h16, wn, bn, wt, bt, wa16, wb16, wo16, lb, norm_g, btab, ln_g, ln_b, tril16, alpha, topk):
    L = h.shape[0]
    tm = min(512, L)
    a_in, g_in, kb, ki, qbt, qit, vt3, wit = _proj(h16, wn, bn, wt, bt, tm)
    ga_in = _hgrn(a_in, g_in, lb[None, :], norm_g[None, :], tril16)
    gb_in = _dsa(qbt, qit, wit, g_in, kb.reshape(L // KB, KB, D_B), vt3,
                 ki.reshape(L // KB, KB, IDX_DIM), btab, topk)
    return _out_block(h, g_in, ga_in, gb_in, wa16, wb16, wo16, ln_g[None, :], ln_b[None, :], alpha, tm)


def kernel(x, w_in, b_in, w_up_a, w_up_b, w_out, lb_logits, norm_a_g, rel_bias, ln_g, ln_b):
    depth = w_in.shape[0]
    batch, L, _ = x.shape
    alpha = (2 * depth) ** 0.25
    topk = min(TOPK_MAX, L // 4)
    lbs = jnp.cumsum(jax.nn.softmax(lb_logits.astype(F32), axis=0), axis=0)
    lbs = lbs - lbs[0:1]
    r = jnp.arange(HG_ROWS)
    tril16 = ((r[:, None] >= r[None, :]) & ((r[:, None] // HG_CHUNK) == (r[None, :] // HG_CHUNK))).astype(BF16)
    btab = _bias_table(rel_bias.astype(F32))
    wn, bn, wt, bt = _prep_in_proj(w_in, b_in)
    wa16, wb16, wo16 = w_up_a.astype(BF16), w_up_b.astype(BF16), w_out.astype(BF16)
    outs = []
    for bi in range(batch):
        h = x[bi]
        h16 = h.astype(BF16)
        for layer in range(depth):
            h, h16 = _layer(h, h16, wn[layer], bn[layer], wt[layer], bt[layer],
                            wa16[layer], wb16[layer], wo16[layer],
                            lbs[layer], norm_a_g[layer], btab, ln_g[layer], ln_b[layer], tril16, alpha, topk)
        outs.append(h)
    return jnp.stack(outs, axis=0)
```

```python
import functools
import math

import jax
import jax.numpy as jnp
from jax import lax
from jax.experimental import pallas as pl
from jax.experimental.pallas import tpu as pltpu

F32 = jnp.float32
BF16 = jnp.bfloat16
I32 = jnp.int32

D_MODEL = 1024
D_A = 512
HEAD_DIM_A = 128
N_HEADS_A = D_A // HEAD_DIM_A
N_HEADS_B = 8
HEAD_DIM_B = 64
D_B = N_HEADS_B * HEAD_DIM_B
IDX_HEADS = 8
IDX_DIM = 64
TOPK_MAX = 256
N_BUCKETS = 32
MAX_DISTANCE = 128
LN_EPS = 1e-5
RMS_EPS = 1e-6

LANES = 128
VMEM_LIMIT_BYTES = 58 * 1024 * 1024

HG_ROWS = 512
HG_CHUNK = 64
HG_SUB = 8

TQ = 128
KB = 128
TS = 1024
TF = 1024
NEAR = 256
MASK_NEG = -1e30
CAND_G = 16
CAND_M = 12
CAND_MIN_TILES = 2
LOG2E = 1.4426950408889634

INT_MIN = -2147483648
NEG_INF_KEY = -2139095041


def _cparams(sem):
    return pltpu.CompilerParams(dimension_semantics=sem, vmem_limit_bytes=VMEM_LIMIT_BYTES)


def _tree_sum(xs):
    xs = list(xs)
    while len(xs) > 1:
        nxt = [xs[a] + xs[a + 1] for a in range(0, len(xs) - 1, 2)]
        if len(xs) % 2:
            nxt.append(xs[-1])
        xs = nxt
    return xs[0]


def _oddeven_merge(lo, hi, r):
    step = r * 2
    if step < hi - lo:
        yield from _oddeven_merge(lo, hi, step)
        yield from _oddeven_merge(lo + r, hi, step)
        yield from [(a, a + r) for a in range(lo + r, hi - r, step)]
    else:
        yield (lo, lo + r)


def _oddeven_sort(lo, hi):
    if hi - lo >= 1:
        mid = lo + (hi - lo) // 2
        yield from _oddeven_sort(lo, mid)
        yield from _oddeven_sort(mid + 1, hi)
        yield from _oddeven_merge(lo, hi, 1)


def _sorted_desc(xs):
    xs = list(xs)
    for a, b in _oddeven_sort(0, len(xs) - 1):
        xs[a], xs[b] = jnp.maximum(xs[a], xs[b]), jnp.minimum(xs[a], xs[b])
    return xs


@functools.lru_cache(maxsize=None)
def _merge_top_plan(m, nb):
    n = 1
    while n < 2 * max(m, nb):
        n *= 2
    pos = [None] * n
    for k in range(m):
        pos[k] = ("l", k)
    for k in range(nb):
        pos[n // 2 + k] = ("b", k)
    ops = []
    for a, b in _oddeven_merge(0, n - 1, 1):
        va, vb = pos[a], pos[b]
        if vb is None:
            continue
        if va is None:
            pos[a], pos[b] = vb, None
            continue
        mx, mn = ("t", len(ops), 0), ("t", len(ops), 1)
        ops.append((mx, mn, va, vb))
        pos[a], pos[b] = mx, mn
    outs = tuple(pos[:m])
    live = set(outs)
    kept = []
    for mx, mn, va, vb in reversed(ops):
        if mx in live or mn in live:
            kept.append((mx if mx in live else None, mn if mn in live else None, va, vb))
            live.update((va, vb))
    return tuple(reversed(kept)), outs


def _merge_top(lst, batch):
    ops, outs = _merge_top_plan(len(lst), len(batch))
    env = {("l", k): x for k, x in enumerate(lst)}
    env.update({("b", k): x for k, x in enumerate(batch)})
    for mx, mn, va, vb in ops:
        if mx is not None:
            env[mx] = jnp.maximum(env[va], env[vb])
        if mn is not None:
            env[mn] = jnp.minimum(env[va], env[vb])
    return [env[o] for o in outs]


PN_A = 3 * D_A
PN_G = D_A + D_B + 2 * D_MODEL
PN = PN_A + PN_G + D_B + IDX_DIM
PT_W = 16
PT = D_B + IDX_HEADS * IDX_DIM + D_B + PT_W
PROJ_CHUNK = 512


def _proj_kernel(x_ref, wn_ref, bn_ref, wt_ref, bt_ref,
                 a_ref, g_ref, kb_ref, ki_ref, qbt_ref, qit_ref, vt_ref, wit_ref):
    x = x_ref[...]
    tm = x.shape[0]

    def rows(c0, n):
        return jnp.dot(x, wn_ref[:, c0:c0 + n], preferred_element_type=F32) + bn_ref[:, c0:c0 + n]

    def cols(r0, n):
        return lax.dot_general(wt_ref[r0:r0 + n, :], x, (((1,), (1,)), ((), ())),
                               preferred_element_type=F32) + bt_ref[r0:r0 + n, :]

    for c in range(PN_A // PROJ_CHUNK):
        a_ref[:, c * PROJ_CHUNK:(c + 1) * PROJ_CHUNK] = rows(c * PROJ_CHUNK, PROJ_CHUNK)
    for c in range(PN_G // PROJ_CHUNK):
        g_ref[:, c * PROJ_CHUNK:(c + 1) * PROJ_CHUNK] = rows(PN_A + c * PROJ_CHUNK, PROJ_CHUNK).astype(BF16)
    kb_ref[...] = rows(PN_A + PN_G, D_B).astype(BF16)
    ki_ref[...] = rows(PN_A + PN_G + D_B, IDX_DIM).astype(BF16)
    qbt_ref[...] = cols(0, D_B).astype(BF16)
    qit_ref[...] = cols(D_B, IDX_HEADS * IDX_DIM).astype(BF16)
    vt = cols(D_B + IDX_HEADS * IDX_DIM, D_B).astype(BF16)
    for c in range(tm // KB):
        vt_ref[c] = vt[:, c * KB:(c + 1) * KB]
    wit_ref[...] = cols(2 * D_B + IDX_HEADS * IDX_DIM, PT_W)


def _proj(x16, wn16, bn, wt16, bt, tm):
    L, k = x16.shape
    row = lambda n: pl.BlockSpec((tm, n), lambda i: (i, 0))
    colT = lambda n: pl.BlockSpec((n, tm), lambda i: (0, i))
    const = lambda shape: pl.BlockSpec(shape, lambda i: (0, 0))
    sds = jax.ShapeDtypeStruct
    return pl.pallas_call(
        _proj_kernel,
        grid=(L // tm,),
        in_specs=[row(k), const((k, PN)), const((1, PN)), const((PT, k)), const((PT, 1))],
        out_specs=[row(PN_A), row(PN_G), row(D_B), row(IDX_DIM),
                   colT(D_B), colT(IDX_HEADS * IDX_DIM),
                   pl.BlockSpec((tm // KB, D_B, KB), lambda i: (i, 0, 0)), colT(PT_W)],
        out_shape=[sds((L, PN_A), F32), sds((L, PN_G), BF16), sds((L, D_B), BF16), sds((L, IDX_DIM), BF16),
                   sds((D_B, L), BF16), sds((IDX_HEADS * IDX_DIM, L), BF16),
                   sds((L // KB, D_B, KB), BF16), sds((PT_W, L), F32)],
        compiler_params=_cparams(("arbitrary",)),
        name="proj",
    )(x16, wn16, bn, wt16, bt)


def _split3_dot(t16, x):
    x1 = x.astype(BF16)
    r1 = x - x1.astype(F32)
    x2 = r1.astype(BF16)
    r2 = r1 - x2.astype(F32)
    x3 = r2.astype(BF16)
    return (jnp.dot(t16, x1, preferred_element_type=F32)
            + jnp.dot(t16, x2, preferred_element_type=F32)
            + jnp.dot(t16, x3, preferred_element_type=F32))


def _hgrn_kernel(lb_ref, g_ref, q_ref, f_ref, i_ref, z_ref, tril_ref, o_ref, st_ref):
    @pl.when(pl.program_id(1) == 0)
    def _():
        st_ref[...] = jnp.zeros_like(st_ref)

    rows = q_ref.shape[0]
    q = q_ref[...]
    fl = f_ref[...]
    v = i_ref[...]
    lb = lb_ref[...]
    log_lb = jnp.log(lb)
    log_1m = jnp.log(1.0 - lb)
    log_sig = jnp.minimum(fl, 0.0) - jnp.log(1.0 + jnp.exp(-jnp.abs(fl)))
    bb = log_1m + log_sig
    mx = jnp.maximum(log_lb, bb)
    log_f = mx + jnp.log(1.0 + jnp.exp(-jnp.abs(log_lb - bb)))
    kk = (1.0 - lb) * jax.nn.sigmoid(-fl)
    b = _split3_dot(tril_ref[...], log_f * LOG2E)

    row_l = lax.broadcasted_iota(I32, (rows, 1), 0) % HG_SUB
    ones16 = jnp.ones((LANES, LANES), BF16)
    qk0 = (q * kk).astype(BF16)
    o = jnp.dot(qk0, ones16, preferred_element_type=F32) * v
    for d in range(1, HG_SUB):
        valid = (row_l + d) < HG_SUB
        qd = pltpu.roll(q, rows - d, 0)
        bd = pltpu.roll(b, rows - d, 0)
        dec = jnp.exp2(jnp.where(valid, bd - b, 0.0))
        e = jnp.where(valid, qd * kk * dec, 0.0).astype(BF16)
        c = jnp.dot(e, ones16, preferred_element_type=F32) * v
        o = o + pltpu.roll(c, d, 0)

    tl = lax.broadcasted_iota(I32, (HG_CHUNK, HG_CHUNK), 0)
    sl = lax.broadcasted_iota(I32, (HG_CHUNK, HG_CHUNK), 1)
    rl = lax.broadcasted_iota(I32, (HG_CHUNK, 1), 0)
    widths = []
    w = HG_SUB
    while w < HG_CHUNK:
        widths.append(w)
        w *= 2
    pair_mask = {w: ((tl // (2 * w)) == (sl // (2 * w))) & ((tl % (2 * w)) >= w) & ((sl % (2 * w)) < w)
                 for w in widths}
    is_query = {w: (rl % (2 * w)) >= w for w in widths}

    def pair_scores(w, qc, kc, bc):
        hi = is_query[w]
        ref = bc[w - 1:w, :]
        for g in range(1, HG_CHUNK // (2 * w)):
            ref = jnp.where((rl // (2 * w)) == g, bc[g * 2 * w + w - 1:g * 2 * w + w, :], ref)
        ql = jnp.where(hi, qc * jnp.exp2(jnp.where(hi, bc - ref, 0.0)), 0.0)
        kl = jnp.where(hi, 0.0, kc * jnp.exp2(jnp.where(hi, 0.0, ref - bc)))
        s = lax.dot_general(ql.astype(BF16), kl.astype(BF16), (((1,), (1,)), ((), ())),
                            preferred_element_type=F32)
        return jnp.where(pair_mask[w], s, 0.0)

    st = st_ref[...]
    outs = []
    for c in range(rows // HG_CHUNK):
        sl_c = slice(c * HG_CHUNK, (c + 1) * HG_CHUNK)
        qc, kc, vc, bc = q[sl_c], kk[sl_c], v[sl_c], b[sl_c]
        b_last = bc[HG_CHUNK - 1:HG_CHUNK, :]
        p = pair_scores(widths[0], qc, kc, bc)
        for w in widths[1:]:
            p = p + pair_scores(w, qc, kc, bc)
        o_c = jnp.dot(p.astype(BF16), vc.astype(BF16), preferred_element_type=F32)
        qe = (qc * jnp.exp2(bc)).astype(BF16)
        o_c = o_c + lax.dot_general(qe, st.astype(BF16), (((1,), (1,)), ((), ())),
                                    preferred_element_type=F32)
        ke = (kc * jnp.exp2(b_last - bc)).astype(BF16)
        upd = lax.dot_general(vc.astype(BF16), ke, (((0,), (0,)), ((), ())),
                              preferred_element_type=F32)
        st = st * jnp.exp2(b_last) + upd
        outs.append(o_c)
    st_ref[...] = st
    o = o + jnp.concatenate(outs, axis=0)

    ms = jnp.mean(o * o, axis=-1, keepdims=True)
    oa = o * lax.rsqrt(ms + RMS_EPS) * g_ref[...]
    z = z_ref[...].astype(F32)
    o_ref[...] = (oa * (z * jax.nn.sigmoid(z))).astype(o_ref.dtype)


def _hgrn(a_in, g_in, lb, norm_g, tril16):
    L = a_in.shape[0]
    nh = N_HEADS_A
    blk = lambda off: pl.BlockSpec((HG_ROWS, HEAD_DIM_A), lambda h, s: (s, off + h))
    return pl.pallas_call(
        _hgrn_kernel,
        grid=(nh, L // HG_ROWS),
        in_specs=[pl.BlockSpec((1, HEAD_DIM_A), lambda h, s: (0, h)),
                  pl.BlockSpec((1, HEAD_DIM_A), lambda h, s: (0, h)),
                  blk(0), blk(nh), blk(2 * nh),
                  pl.BlockSpec((HG_ROWS, HEAD_DIM_A), lambda h, s: (s, h)),
                  pl.BlockSpec((HG_ROWS, HG_ROWS), lambda h, s: (0, 0))],
        out_specs=pl.BlockSpec((HG_ROWS, HEAD_DIM_A), lambda h, s: (s, h)),
        out_shape=jax.ShapeDtypeStruct((L, D_A), BF16),
        scratch_shapes=[pltpu.VMEM((HEAD_DIM_A, HEAD_DIM_A), F32)],
        compiler_params=_cparams(("arbitrary", "arbitrary")),
        name="hgrn2",
    )(lb, norm_g, a_in, a_in, a_in, g_in, tril16)


def _bias_table_kernel(rb_ref, o_ref):
    n_c = o_ref.shape[1]
    c = lax.broadcasted_iota(I32, (n_c, TQ), 0)
    t = lax.broadcasted_iota(I32, (n_c, TQ), 1)
    dist = t + KB - c
    max_exact = N_BUCKETS // 2
    d = jnp.maximum(dist, 0)
    df = jnp.maximum(d, 1).astype(F32)
    large = max_exact + (jnp.log(df / max_exact) / math.log(MAX_DISTANCE / max_exact)
                         * (N_BUCKETS - max_exact)).astype(I32)
    large = jnp.minimum(large, N_BUCKETS - 1)
    bucket = jnp.where(d < max_exact, d, large)
    for h in range(N_HEADS_B):
        acc = jnp.zeros((n_c, TQ), F32)
        for k in range(N_BUCKETS):
            acc = jnp.where(bucket == k, rb_ref[k, h], acc)
        o_ref[h] = (acc - rb_ref[N_BUCKETS - 1, h]) * LOG2E


def _bias_table(rel_bias):
    return pl.pallas_call(
        _bias_table_kernel,
        in_specs=[pl.BlockSpec(memory_space=pltpu.SMEM)],
        out_specs=pl.BlockSpec(memory_space=pltpu.VMEM),
        out_shape=jax.ShapeDtypeStruct((N_HEADS_B, NEAR + KB, TQ), F32),
        name="t5_bias_table",
    )(rel_bias)


def _dsa_kernel(qbt_ref, qit_ref, wit_ref, zb_ref, kb_ref, vt_ref, ki_ref, bt_ref, o_ref,
                key_ref, cand_ref, candk_ref, stat_ref, lg_ref, mt_ref, acc_ref, m_ref, l_ref,
                *, topk):
    i = pl.program_id(0)
    t_idx = i * TQ + lax.broadcasted_iota(I32, (1, TQ), 1)
    n_kb = TS // KB
    n_fb = TF // KB
    nt = (TF // TS) * ((i + n_fb) // n_fb)
    row_ts = lax.broadcasted_iota(I32, (TS, 1), 0)

    qit = qit_ref[...]
    w = wit_ref[...] * (IDX_HEADS ** -0.5)
    qi_pairs = [jnp.concatenate([qit[2 * p * IDX_DIM:(2 * p + 1) * IDX_DIM, :],
                                 qit[(2 * p + 1) * IDX_DIM:(2 * p + 2) * IDX_DIM, :]], axis=1)
                for p in range(IDX_HEADS // 2)]

    def score_tile(j, carry):
        kt = ki_ref[pl.ds(j * n_kb, n_kb)].reshape(TS, IDX_DIM)
        acc = jnp.zeros((TS, TQ), F32)
        for p in range(IDX_HEADS // 2):
            dd = jnp.dot(kt, qi_pairs[p], preferred_element_type=F32)
            acc = acc + w[2 * p:2 * p + 1, :] * jnp.maximum(dd[:, :TQ], 0.0)
            acc = acc + w[2 * p + 1:2 * p + 2, :] * jnp.maximum(dd[:, TQ:], 0.0)
        s_idx = j * TS + row_ts
        sc = jnp.where(s_idx <= t_idx, acc, -jnp.inf)
        sc = sc + 0.0
        bits = pltpu.bitcast(sc, I32)
        keys = bits ^ ((bits >> 31) & 0x7FFFFFFF)
        key_ref[pl.ds(j * n_kb, n_kb)] = keys.reshape(n_kb, KB, TQ)
        sc3 = sc.reshape(TS // 8, 8, TQ)
        for g in range(CAND_G):
            lst = [cand_ref[g, k] for k in range(CAND_M)]
            batch = _sorted_desc([sc3[g + CAND_G * c] for c in range(TS // 8 // CAND_G)])
            lst = _merge_top(lst, batch)
            for k in range(CAND_M):
                cand_ref[g, k] = lst[k]
        return carry

    cand_ref[...] = jnp.full(cand_ref.shape, -jnp.inf, F32)
    lax.fori_loop(0, nt, score_tile, 0)

    def to_keys(sc):
        bits = pltpu.bitcast(sc, I32)
        return bits ^ ((bits >> 31) & 0x7FFFFFFF)

    def count_ge(thr_signed):
        def body(j, cnt):
            keys = key_ref[pl.ds(j * n_kb, n_kb)].reshape(TS // 8, 8, TQ)
            return cnt + _tree_sum([jnp.where(keys[k] >= thr_signed, 1, 0) for k in range(TS // 8)])
        cnt = lax.fori_loop(0, nt, body, jnp.zeros((8, TQ), I32))
        return jnp.sum(cnt, axis=0, keepdims=True)

    def bisect(count_fn, prefix=None, nbits=32):
        def bit_step(bi, tb):
            cand = tb | jnp.left_shift(jnp.int32(1), nbits - 1 - bi)
            return jnp.where(count_fn(cand ^ INT_MIN) >= topk, cand, tb)
        tb = lax.fori_loop(0, nbits, bit_step, jnp.zeros((1, TQ), I32) if prefix is None else prefix)
        return jnp.maximum(tb ^ INT_MIN, NEG_INF_KEY + 1)

    stat_ref[2] = jnp.ones((8, TQ), I32)

    @pl.when(nt > CAND_MIN_TILES)
    def _():
        candk_ref[...] = to_keys(cand_ref[...].reshape(CAND_G * CAND_M, 8, TQ))

        def count_cand_ge(thr_signed):
            hits = [jnp.where(candk_ref[k] >= thr_signed, 1, 0) for k in range(CAND_G * CAND_M)]
            return jnp.sum(_tree_sum(hits), axis=0, keepdims=True)

        n_cls = 8 * CAND_G
        rank = -(-topk // n_cls)
        if rank <= CAND_M:
            rows_r = [candk_ref[g * CAND_M + rank - 1] for g in range(CAND_G)]
            hi_b = jnp.max(functools.reduce(jnp.maximum, rows_r), axis=0, keepdims=True) ^ INT_MIN
            lo_b = jnp.min(functools.reduce(jnp.minimum, rows_r), axis=0, keepdims=True) ^ INT_MIN
            diff = hi_b ^ lo_b
            expo = (pltpu.bitcast(diff.astype(F32), I32) >> 23) & 0xFF
            bits = jnp.where(diff < 0, 32, jnp.where(diff == 0, 0, expo - 126))
            nbits = jnp.minimum(jnp.max(bits), 32)
            low = jnp.where(nbits >= 32, jnp.int32(-1),
                            jnp.left_shift(jnp.int32(1), jnp.minimum(nbits, 31)) - 1)
            thr_c = bisect(count_cand_ge, hi_b & ~low, nbits)
        else:
            thr_c = bisect(count_cand_ge)
        stat_ref[0] = jnp.broadcast_to(thr_c, (8, TQ))
        stat_ref[1] = jnp.broadcast_to(count_ge(thr_c), (8, TQ))
        stat_ref[2] = jnp.broadcast_to((count_ge(thr_c + 1) >= topk).astype(I32), (8, TQ))

    @pl.when(jnp.max(stat_ref[2]) > 0)
    def _():
        thr_f = bisect(count_ge)
        stat_ref[0] = jnp.broadcast_to(thr_f, (8, TQ))
        stat_ref[1] = jnp.broadcast_to(count_ge(thr_f), (8, TQ))

    thr = stat_ref[0, 0:1, :]
    cnt_ge_thr = stat_ref[1, 0:1, :]
    excess = (cnt_ge_thr > topk)

    @pl.when(jnp.max(excess.astype(I32)) > 0)
    def _():
        need = topk - count_ge(thr + 1)
        sub = lax.broadcasted_iota(I32, (8, 1), 0)

        def demote(j, seen):
            keys = key_ref[pl.ds(j * n_kb, n_kb)].reshape(TS // 8, 8, TQ)
            out = []
            for v in range(TS // 8):
                kv = keys[v]
                eq = kv == thr
                p = jnp.where(eq, 1, 0)
                for sh in (1, 2, 4):
                    p = p + jnp.where(sub >= sh, pltpu.roll(p, sh, 0), 0)
                drop = excess & eq & (seen + p > need)
                out.append(jnp.where(drop, kv - 1, kv))
                seen = seen + p[7:8, :]
            key_ref[pl.ds(j * n_kb, n_kb)] = jnp.stack(out, axis=0).reshape(n_kb, KB, TQ)
            return seen
        lax.fori_loop(0, nt, demote, jnp.zeros((1, TQ), I32))

    qbt = qbt_ref[...]
    zq = jnp.zeros((HEAD_DIM_B, TQ), BF16)
    qb_pairs = [jnp.concatenate(
        [jnp.concatenate([qbt[2 * p * HEAD_DIM_B:(2 * p + 1) * HEAD_DIM_B, :], zq], axis=1),
         jnp.concatenate([zq, qbt[(2 * p + 1) * HEAD_DIM_B:(2 * p + 2) * HEAD_DIM_B, :]], axis=1)],
        axis=0) for p in range(N_HEADS_B // 2)]

    m_ref[...] = jnp.full(m_ref.shape, MASK_NEG, F32)
    l_ref[...] = jnp.zeros(l_ref.shape, F32)
    acc_ref[...] = jnp.zeros(acc_ref.shape, F32)

    def stage1(kt, madd, bias_of_head):
        n = kt.shape[0]
        for p in range(N_HEADS_B // 2):
            lg = jnp.dot(kt[:, 2 * p * HEAD_DIM_B:(2 * p + 2) * HEAD_DIM_B], qb_pairs[p],
                         preferred_element_type=F32)
            for hh in range(2):
                h = 2 * p + hh
                lo = lg[:, hh * TQ:(hh + 1) * TQ] + madd
                if bias_of_head is not None:
                    lo = lo + bias_of_head(h)
                lg_ref[h, 0:n, :] = lo
                mt_ref[h] = jnp.broadcast_to(jnp.max(lo, axis=0, keepdims=True), (8, TQ))

    def stage2(vt):
        n = vt.shape[1]
        for h in range(N_HEADS_B):
            m_old = m_ref[h]
            m_new = jnp.maximum(m_old, mt_ref[h])
            alpha = jnp.exp2(m_old - m_new)
            m_ref[h] = m_new
            pe = jnp.exp2(lg_ref[h, 0:n, :] - m_new[0:1, :]).astype(BF16)
            rs = slice(h * HEAD_DIM_B, (h + 1) * HEAD_DIM_B)
            vt_aug = jnp.concatenate([vt[rs, :], jnp.ones((16, n), BF16)], axis=0)
            pv = jnp.dot(vt_aug, pe, preferred_element_type=F32)
            l_ref[h] = alpha * l_ref[h] + pv[HEAD_DIM_B:HEAD_DIM_B + 8, :]
            acc_ref[rs, :] = alpha[0:1, :] * acc_ref[rs, :] + pv[0:HEAD_DIM_B, :]

    far_end = (i - 1) * KB
    nf = (i + n_fb - 2) // n_fb
    row_tf = lax.broadcasted_iota(I32, (TF, 1), 0)

    def far_tile(j, carry):
        kt = kb_ref[pl.ds(j * n_fb, n_fb)].reshape(TF, D_B)
        keys = key_ref[pl.ds(j * n_fb, n_fb)].reshape(TF, TQ)
        s_idx = j * TF + row_tf
        madd = jnp.where((keys >= thr) & (s_idx < far_end), 0.0, MASK_NEG)
        stage1(kt, madd, None)
        stage2(jnp.concatenate([vt_ref[j * n_fb + c] for c in range(n_fb)], axis=1))
        return carry
    lax.fori_loop(0, nf, far_tile, 0)

    nb0 = jnp.maximum(i - 1, 0)
    toff = jnp.where(i == 0, KB, 0)
    kt = kb_ref[pl.ds(nb0, NEAR // KB)].reshape(NEAR, D_B)
    keys = key_ref[pl.ds(nb0, NEAR // KB)].reshape(NEAR, TQ)
    madd = jnp.where(keys >= thr, 0.0, MASK_NEG)
    stage1(kt, madd, lambda h: bt_ref[h, pl.ds(pl.multiple_of(toff, KB), NEAR), :])
    stage2(jnp.concatenate([vt_ref[nb0 + c] for c in range(NEAR // KB)], axis=1))

    ot = jnp.concatenate([acc_ref[h * HEAD_DIM_B:(h + 1) * HEAD_DIM_B, :] * (1.0 / l_ref[h, 0:1, :])
                          for h in range(N_HEADS_B)], axis=0)
    ob = ot.T
    z = zb_ref[...].astype(F32)
    o_ref[...] = (ob * (z * jax.nn.sigmoid(z))).astype(o_ref.dtype)


def _dsa(qbt, qit, wit, g_in, kb3, vt3, ki3, btab, topk):
    L = qbt.shape[1]
    n_blk = L // KB
    whole = pl.BlockSpec(memory_space=pltpu.VMEM)
    return pl.pallas_call(
        functools.partial(_dsa_kernel, topk=topk),
        grid=(L // TQ,),
        in_specs=[pl.BlockSpec((D_B, TQ), lambda i: (0, i)),
                  pl.BlockSpec((IDX_HEADS * IDX_DIM, TQ), lambda i: (0, i)),
                  pl.BlockSpec((16, TQ), lambda i: (0, i)),
                  pl.BlockSpec((TQ, D_B), lambda i: (i, 1)),
                  whole, whole, whole, whole],
        out_specs=pl.BlockSpec((TQ, D_B), lambda i: (i, 0)),
        out_shape=jax.ShapeDtypeStruct((L, D_B), BF16),
        scratch_shapes=[pltpu.VMEM((n_blk, KB, TQ), I32),
                        pltpu.VMEM((CAND_G, CAND_M, 8, TQ), F32),
                        pltpu.VMEM((CAND_G * CAND_M, 8, TQ), I32),
                        pltpu.VMEM((3, 8, TQ), I32),
                        pltpu.VMEM((N_HEADS_B, TF, TQ), F32),
                        pltpu.VMEM((N_HEADS_B, 8, TQ), F32),
                        pltpu.VMEM((D_B, TQ), F32),
                        pltpu.VMEM((N_HEADS_B, 8, TQ), F32),
                        pltpu.VMEM((N_HEADS_B, 8, TQ), F32)],
        compiler_params=_cparams(("arbitrary",)),
        name="dsa",
    )(qbt, qit, wit, g_in, kb3, vt3, ki3, btab)


def _out_kernel(x_ref, ga_ref, gb_ref, a_ref, bq_ref, wa_ref, wb_ref, wo_ref, lg_ref, lb_ref,
                o_ref, o16_ref, *, alpha):
    ua = jnp.dot(a_ref[...], wa_ref[...], preferred_element_type=F32)
    ub = jnp.dot(bq_ref[...], wb_ref[...], preferred_element_type=F32)
    merged = (jax.nn.sigmoid(ga_ref[...].astype(F32)) * ua
              + jax.nn.sigmoid(gb_ref[...].astype(F32)) * ub)
    y = jnp.dot(merged.astype(BF16), wo_ref[...], preferred_element_type=F32)
    r = alpha * x_ref[...] + y
    mu = jnp.mean(r, axis=-1, keepdims=True)
    var = jnp.mean(jnp.square(r - mu), axis=-1, keepdims=True)
    out = (r - mu) * lax.rsqrt(var + LN_EPS) * lg_ref[...] + lb_ref[...]
    o_ref[...] = out
    o16_ref[...] = out.astype(BF16)


def _out_block(x, g_in, ga_in, gb_in, wa16, wb16, wo16, ln_g, ln_b, alpha, tm):
    L = x.shape[0]
    const = lambda shape: pl.BlockSpec(shape, lambda i: (0, 0))
    return pl.pallas_call(
        functools.partial(_out_kernel, alpha=alpha),
        grid=(L // tm,),
        in_specs=[pl.BlockSpec((tm, D_MODEL), lambda i: (i, 0)),
                  pl.BlockSpec((tm, D_MODEL), lambda i: (i, 1)),
                  pl.BlockSpec((tm, D_MODEL), lambda i: (i, 2)),
                  pl.BlockSpec((tm, D_A), lambda i: (i, 0)),
                  pl.BlockSpec((tm, D_B), lambda i: (i, 0)),
                  const((D_A, D_MODEL)), const((D_B, D_MODEL)), const((D_MODEL, D_MODEL)),
                  const((1, D_MODEL)), const((1, D_MODEL))],
        out_specs=[pl.BlockSpec((tm, D_MODEL), lambda i: (i, 0)),
                   pl.BlockSpec((tm, D_MODEL), lambda i: (i, 0))],
        out_shape=[jax.ShapeDtypeStruct((L, D_MODEL), F32),
                   jax.ShapeDtypeStruct((L, D_MODEL), BF16)],
        compiler_params=_cparams(("arbitrary",)),
        name="out_block",
    )(x, g_in, g_in, ga_in, gb_in, wa16, wb16, wo16, ln_g, ln_b)


def _prep_in_proj(w_in, b_in):
    c0 = 4 * D_A
    ci = c0 + 4 * D_B
    cw = ci + IDX_HEADS * IDX_DIM + IDX_DIM
    cg = cw + IDX_HEADS
    row_cols = [(0, 4 * D_A), (c0 + 3 * D_B, D_B), (cg, 2 * D_MODEL), (c0 + D_B, D_B),
                (ci + IDX_HEADS * IDX_DIM, IDX_DIM)]
    wn = jnp.concatenate([w_in[:, :, a:a + n] for a, n in row_cols], axis=2).astype(BF16)
    bn = jnp.concatenate([b_in[:, a:a + n] for a, n in row_cols], axis=1)[:, None, :]
    t_cols = [(c0, D_B, HEAD_DIM_B ** -0.5 * LOG2E), (ci, IDX_HEADS * IDX_DIM, IDX_DIM ** -0.5),
              (c0 + 2 * D_B, D_B, 1.0), (cw, IDX_HEADS, 1.0)]
    pad = PT_W - IDX_HEADS
    wt = jnp.concatenate([w_in[:, :, a:a + n] * s for a, n, s in t_cols], axis=2)
    wt = jnp.pad(wt, ((0, 0), (0, 0), (0, pad))).transpose(0, 2, 1).astype(BF16)
    bt = jnp.concatenate([b_in[:, a:a + n] * s for a, n, s in t_cols], axis=1)
    bt = jnp.pad(bt, ((0, 0), (0, pad)))[:, :, None]
    return wn, bn, wt, bt


def _layer(h, h16, wn, bn, wt, bt, wa16, wb16, wo16, lb, norm_g, btab, ln_g, ln_b, tril16, alpha, topk):
    L = h.shape[0]
    tm = min(512, L)
    a_in, g_in, kb, ki, qbt, qit, vt3, wit = _proj(h16, wn, bn, wt, bt, tm)
    ga_in = _hgrn(a_in, g_in, lb[None, :], norm_g[None, :], tril16)
    gb_in = _dsa(qbt, qit, wit, g_in, kb.reshape(L // KB, KB, D_B), vt3,
                 ki.reshape(L // KB, KB, IDX_DIM), btab, topk)
    return _out_block(h, g_in, ga_in, gb_in, wa16, wb16, wo16, ln_g[None, :], ln_b[None, :], alpha, tm)


def kernel(x, w_in, b_in, w_up_a, w_up_b, w_out, lb_logits, norm_a_g, rel_bias, ln_g, ln_b):
    depth = w_in.shape[0]
    batch, L, _ = x.shape
    alpha = (2 * depth) ** 0.25
    topk = min(TOPK_MAX, L // 4)
    lbs = jnp.cumsum(jax.nn.softmax(lb_logits.astype(F32), axis=0), axis=0)
    lbs = lbs - lbs[0:1]
    r = jnp.arange(HG_ROWS)
    tril16 = ((r[:, None] >= r[None, :]) & ((r[:, None] // HG_CHUNK) == (r[None, :] // HG_CHUNK))).astype(BF16)
    btab = _bias_table(rel_bias.astype(F32))
    wn, bn, wt, bt = _prep_in_proj(w_in, b_in)
    wa16, wb16, wo16 = w_up_a.astype(BF16), w_up_b.astype(BF16), w_out.astype(BF16)
    outs = []
    for bi in range(batch):
        h = x[bi]
        h16 = h.astype(BF16)
        for layer in range(depth):
            h, h16 = _layer(h, h16, wn[layer], bn[layer], wt[layer], bt[layer],
                            wa16[layer], wb16[layer], wo16[layer],
                            lbs[layer], norm_a_g[layer], btab, ln_g[layer], ln_b[layer], tril16, alpha, topk)
        outs.append(h)
    return jnp.stack(outs, axis=0)
```

```python
import functools
import math

import jax
import jax.numpy as jnp
from jax import lax
from jax.experimental import pallas as pl
from jax.experimental.pallas import tpu as pltpu

F32 = jnp.float32
BF16 = jnp.bfloat16
I32 = jnp.int32

D_MODEL = 1024
D_A = 512
HEAD_DIM_A = 128
N_HEADS_A = D_A // HEAD_DIM_A
N_HEADS_B = 8
HEAD_DIM_B = 64
D_B = N_HEADS_B * HEAD_DIM_B
IDX_HEADS = 8
IDX_DIM = 64
TOPK_MAX = 256
N_BUCKETS = 32
MAX_DISTANCE = 128
LN_EPS = 1e-5
RMS_EPS = 1e-6

LANES = 128
VMEM_LIMIT_BYTES = 58 * 1024 * 1024

HG_ROWS = 512
HG_CHUNK = 64
HG_SUB = 8

TQ = 128
KB = 128
TS = 1024
TF = 1024
NEAR = 256
MASK_NEG = -1e30
CAND_G = 16
CAND_M = 12
CAND_MIN_TILES = 2
LOG2E = 1.4426950408889634

INT_MIN = -2147483648
NEG_INF_KEY = -2139095041


def _cparams(sem):
    return pltpu.CompilerParams(dimension_semantics=sem, vmem_limit_bytes=VMEM_LIMIT_BYTES)


def _tree_sum(xs):
    xs = list(xs)
    while len(xs) > 1:
        nxt = [xs[a] + xs[a + 1] for a in range(0, len(xs) - 1, 2)]
        if len(xs) % 2:
            nxt.append(xs[-1])
        xs = nxt
    return xs[0]


def _oddeven_merge(lo, hi, r):
    step = r * 2
    if step < hi - lo:
        yield from _oddeven_merge(lo, hi, step)
        yield from _oddeven_merge(lo + r, hi, step)
        yield from [(a, a + r) for a in range(lo + r, hi - r, step)]
    else:
        yield (lo, lo + r)


def _oddeven_sort(lo, hi):
    if hi - lo >= 1:
        mid = lo + (hi - lo) // 2
        yield from _oddeven_sort(lo, mid)
        yield from _oddeven_sort(mid + 1, hi)
        yield from _oddeven_merge(lo, hi, 1)


def _sorted_desc(xs):
    xs = list(xs)
    for a, b in _oddeven_sort(0, len(xs) - 1):
        xs[a], xs[b] = jnp.maximum(xs[a], xs[b]), jnp.minimum(xs[a], xs[b])
    return xs


@functools.lru_cache(maxsize=None)
def _merge_top_plan(m, nb):
    n = 1
    while n < 2 * max(m, nb):
        n *= 2
    pos = [None] * n
    for k in range(m):
        pos[k] = ("l", k)
    for k in range(nb):
        pos[n // 2 + k] = ("b", k)
    ops = []
    for a, b in _oddeven_merge(0, n - 1, 1):
        va, vb = pos[a], pos[b]
        if vb is None:
            continue
        if va is None:
            pos[a], pos[b] = vb, None
            continue
        mx, mn = ("t", len(ops), 0), ("t", len(ops), 1)
        ops.append((mx, mn, va, vb))
        pos[a], pos[b] = mx, mn
    outs = tuple(pos[:m])
    live = set(outs)
    kept = []
    for mx, mn, va, vb in reversed(ops):
        if mx in live or mn in live:
            kept.append((mx if mx in live else None, mn if mn in live else None, va, vb))
            live.update((va, vb))
    return tuple(reversed(kept)), outs


def _merge_top(lst, batch):
    ops, outs = _merge_top_plan(len(lst), len(batch))
    env = {("l", k): x for k, x in enumerate(lst)}
    env.update({("b", k): x for k, x in enumerate(batch)})
    for mx, mn, va, vb in ops:
        if mx is not None:
            env[mx] = jnp.maximum(env[va], env[vb])
        if mn is not None:
            env[mn] = jnp.minimum(env[va], env[vb])
    return [env[o] for o in outs]


PN_A = 3 * D_A
PN_G = D_A + D_B + 2 * D_MODEL
PN = PN_A + PN_G + D_B + IDX_DIM
PT_W = 16
PT = D_B + IDX_HEADS * IDX_DIM + D_B + PT_W
PROJ_CHUNK = 512


def _proj_kernel(x_ref, wn_ref, bn_ref, wt_ref, bt_ref,
                 a_ref, g_ref, kb_ref, ki_ref, qbt_ref, qit_ref, vt_ref, wit_ref):
    x = x_ref[...]
    tm = x.shape[0]

    def rows(c0, n):
        return jnp.dot(x, wn_ref[:, c0:c0 + n], preferred_element_type=F32) + bn_ref[:, c0:c0 + n]

    def cols(r0, n):
        return lax.dot_general(wt_ref[r0:r0 + n, :], x, (((1,), (1,)), ((), ())),
                               preferred_element_type=F32) + bt_ref[r0:r0 + n, :]

    for c in range(PN_A // PROJ_CHUNK):
        a_ref[:, c * PROJ_CHUNK:(c + 1) * PROJ_CHUNK] = rows(c * PROJ_CHUNK, PROJ_CHUNK)
    for c in range(PN_G // PROJ_CHUNK):
        g_ref[:, c * PROJ_CHUNK:(c + 1) * PROJ_CHUNK] = rows(PN_A + c * PROJ_CHUNK, PROJ_CHUNK).astype(BF16)
    kb_ref[...] = rows(PN_A + PN_G, D_B).astype(BF16)
    ki_ref[...] = rows(PN_A + PN_G + D_B, IDX_DIM).astype(BF16)
    qbt_ref[...] = cols(0, D_B).astype(BF16)
    qit_ref[...] = cols(D_B, IDX_HEADS * IDX_DIM).astype(BF16)
    vt = cols(D_B + IDX_HEADS * IDX_DIM, D_B).astype(BF16)
    for c in range(tm // KB):
        vt_ref[c] = vt[:, c * KB:(c + 1) * KB]
    wit_ref[...] = cols(2 * D_B + IDX_HEADS * IDX_DIM, PT_W)


def _proj(x16, wn16, bn, wt16, bt, tm):
    L, k = x16.shape
    row = lambda n: pl.BlockSpec((tm, n), lambda i: (i, 0))
    colT = lambda n: pl.BlockSpec((n, tm), lambda i: (0, i))
    const = lambda shape: pl.BlockSpec(shape, lambda i: (0, 0))
    sds = jax.ShapeDtypeStruct
    return pl.pallas_call(
        _proj_kernel,
        grid=(L // tm,),
        in_specs=[row(k), const((k, PN)), const((1, PN)), const((PT, k)), const((PT, 1))],
        out_specs=[row(PN_A), row(PN_G), row(D_B), row(IDX_DIM),
                   colT(D_B), colT(IDX_HEADS * IDX_DIM),
                   pl.BlockSpec((tm // KB, D_B, KB), lambda i: (i, 0, 0)), colT(PT_W)],
        out_shape=[sds((L, PN_A), F32), sds((L, PN_G), BF16), sds((L, D_B), BF16), sds((L, IDX_DIM), BF16),
                   sds((D_B, L), BF16), sds((IDX_HEADS * IDX_DIM, L), BF16),
                   sds((L // KB, D_B, KB), BF16), sds((PT_W, L), F32)],
        compiler_params=_cparams(("arbitrary",)),
        name="proj",
    )(x16, wn16, bn, wt16, bt)


def _split3_dot(t16, x):
    x1 = x.astype(BF16)
    r1 = x - x1.astype(F32)
    x2 = r1.astype(BF16)
    r2 = r1 - x2.astype(F32)
    x3 = r2.astype(BF16)
    return (jnp.dot(t16, x1, preferred_element_type=F32)
            + jnp.dot(t16, x2, preferred_element_type=F32)
            + jnp.dot(t16, x3, preferred_element_type=F32))


def _hgrn_kernel(lb_ref, g_ref, q_ref, f_ref, i_ref, z_ref, tril_ref, o_ref, st_ref):
    @pl.when(pl.program_id(1) == 0)
    def _():
        st_ref[...] = jnp.zeros_like(st_ref)

    rows = q_ref.shape[0]
    lb = lb_ref[...]
    log_lb = jnp.log(lb)
    log_1m = jnp.log(1.0 - lb)
    fl = f_ref[...]
    log_sig = jnp.minimum(fl, 0.0) - jnp.log(1.0 + jnp.exp(-jnp.abs(fl)))
    bb = log_1m + log_sig
    mx = jnp.maximum(log_lb, bb)
    log_f = mx + jnp.log(1.0 + jnp.exp(-jnp.abs(log_lb - bb)))
    kk = (1.0 - lb) * jax.nn.sigmoid(-fl)
    b = _split3_dot(tril_ref[...], log_f * LOG2E)

    ones16 = jnp.ones((LANES, LANES), BF16)
    row_l = lax.broadcasted_iota(I32, (HG_CHUNK, 1), 0) % HG_SUB

    def diag_blocks(qc, kc, vc, bc):
        es = [(qc * kc).astype(BF16)]
        for d in range(1, HG_SUB):
            valid = (row_l + d) < HG_SUB
            qd = pltpu.roll(qc, HG_CHUNK - d, 0)
            bd = pltpu.roll(bc, HG_CHUNK - d, 0)
            dec = jnp.exp2(jnp.where(valid, bd - bc, 0.0))
            es.append(jnp.where(valid, qd * kc * dec, 0.0).astype(BF16))
        rs = jnp.dot(jnp.concatenate(es, axis=0), ones16, preferred_element_type=F32)
        o_d = rs[0:HG_CHUNK] * vc
        for d in range(1, HG_SUB):
            o_d = o_d + pltpu.roll(rs[d * HG_CHUNK:(d + 1) * HG_CHUNK] * vc, d, 0)
        return o_d

    tl = lax.broadcasted_iota(I32, (HG_CHUNK, HG_CHUNK), 0)
    sl = lax.broadcasted_iota(I32, (HG_CHUNK, HG_CHUNK), 1)
    rl = lax.broadcasted_iota(I32, (HG_CHUNK, 1), 0)
    widths = []
    w = HG_SUB
    while w < HG_CHUNK:
        widths.append(w)
        w *= 2
    pair_mask = {w: ((tl // (2 * w)) == (sl // (2 * w))) & ((tl % (2 * w)) >= w) & ((sl % (2 * w)) < w)
                 for w in widths}
    is_query = {w: (rl % (2 * w)) >= w for w in widths}

    def pair_scores(w, qc, kc, bc):
        hi = is_query[w]
        ref = bc[w - 1:w, :]
        for g in range(1, HG_CHUNK // (2 * w)):
            ref = jnp.where((rl // (2 * w)) == g, bc[g * 2 * w + w - 1:g * 2 * w + w, :], ref)
        ql = jnp.where(hi, qc * jnp.exp2(jnp.where(hi, bc - ref, 0.0)), 0.0)
        kl = jnp.where(hi, 0.0, kc * jnp.exp2(jnp.where(hi, 0.0, ref - bc)))
        s = lax.dot_general(ql.astype(BF16), kl.astype(BF16), (((1,), (1,)), ((), ())),
                            preferred_element_type=F32)
        return jnp.where(pair_mask[w], s, 0.0)

    st = st_ref[...]
    for c in range(rows // HG_CHUNK):
        sl_c = slice(c * HG_CHUNK, (c + 1) * HG_CHUNK)
        qc, vc = q_ref[sl_c, :], i_ref[sl_c, :]
        kc, bc = kk[sl_c], b[sl_c]
        b_last = bc[HG_CHUNK - 1:HG_CHUNK, :]
        p = pair_scores(widths[0], qc, kc, bc)
        for w in widths[1:]:
            p = p + pair_scores(w, qc, kc, bc)
        o_c = jnp.dot(p.astype(BF16), vc.astype(BF16), preferred_element_type=F32)
        o_c = o_c + diag_blocks(qc, kc, vc, bc)
        qe = (qc * jnp.exp2(bc)).astype(BF16)
        o_c = o_c + lax.dot_general(qe, st.astype(BF16), (((1,), (1,)), ((), ())),
                                    preferred_element_type=F32)
        ke = (kc * jnp.exp2(b_last - bc)).astype(BF16)
        upd = lax.dot_general(vc.astype(BF16), ke, (((0,), (0,)), ((), ())),
                              preferred_element_type=F32)
        st = st * jnp.exp2(b_last) + upd
        ms = jnp.mean(o_c * o_c, axis=-1, keepdims=True)
        oa = o_c * lax.rsqrt(ms + RMS_EPS) * g_ref[...]
        z = z_ref[sl_c, :].astype(F32)
        o_ref[sl_c, :] = (oa * (z * jax.nn.sigmoid(z))).astype(o_ref.dtype)
    st_ref[...] = st


def _hgrn(a_in, g_in, lb, norm_g, tril16):
    L = a_in.shape[0]
    nh = N_HEADS_A
    blk = lambda off: pl.BlockSpec((HG_ROWS, HEAD_DIM_A), lambda h, s: (s, off + h))
    return pl.pallas_call(
        _hgrn_kernel,
        grid=(nh, L // HG_ROWS),
        in_specs=[pl.BlockSpec((1, HEAD_DIM_A), lambda h, s: (0, h)),
                  pl.BlockSpec((1, HEAD_DIM_A), lambda h, s: (0, h)),
                  blk(0), blk(nh), blk(2 * nh),
                  pl.BlockSpec((HG_ROWS, HEAD_DIM_A), lambda h, s: (s, h)),
                  pl.BlockSpec((HG_ROWS, HG_ROWS), lambda h, s: (0, 0))],
        out_specs=pl.BlockSpec((HG_ROWS, HEAD_DIM_A), lambda h, s: (s, h)),
        out_shape=jax.ShapeDtypeStruct((L, D_A), BF16),
        scratch_shapes=[pltpu.VMEM((HEAD_DIM_A, HEAD_DIM_A), F32)],
        compiler_params=_cparams(("arbitrary", "arbitrary")),
        name="hgrn2",
    )(lb, norm_g, a_in, a_in, a_in, g_in, tril16)


def _bias_table_kernel(rb_ref, o_ref):
    n_c = o_ref.shape[1]
    c = lax.broadcasted_iota(I32, (n_c, TQ), 0)
    t = lax.broadcasted_iota(I32, (n_c, TQ), 1)
    dist = t + KB - c
    max_exact = N_BUCKETS // 2
    d = jnp.maximum(dist, 0)
    df = jnp.maximum(d, 1).astype(F32)
    large = max_exact + (jnp.log(df / max_exact) / math.log(MAX_DISTANCE / max_exact)
                         * (N_BUCKETS - max_exact)).astype(I32)
    large = jnp.minimum(large, N_BUCKETS - 1)
    bucket = jnp.where(d < max_exact, d, large)
    for h in range(N_HEADS_B):
        acc = jnp.zeros((n_c, TQ), F32)
        for k in range(N_BUCKETS):
            acc = jnp.where(bucket == k, rb_ref[k, h], acc)
        o_ref[h] = (acc - rb_ref[N_BUCKETS - 1, h]) * LOG2E


def _bias_table(rel_bias):
    return pl.pallas_call(
        _bias_table_kernel,
        in_specs=[pl.BlockSpec(memory_space=pltpu.SMEM)],
        out_specs=pl.BlockSpec(memory_space=pltpu.VMEM),
        out_shape=jax.ShapeDtypeStruct((N_HEADS_B, NEAR + KB, TQ), F32),
        name="t5_bias_table",
    )(rel_bias)


def _dsa_kernel(qbt_ref, qit_ref, wit_ref, zb_ref, kb_ref, vt_ref, ki_ref, bt_ref, o_ref,
                key_ref, cand_ref, candk_ref, stat_ref, lg_ref, mt_ref, acc_ref, m_ref, l_ref,
                *, topk):
    i = pl.program_id(0)
    t_idx = i * TQ + lax.broadcasted_iota(I32, (1, TQ), 1)
    n_kb = TS // KB
    n_fb = TF // KB
    nt = (TF // TS) * ((i + n_fb) // n_fb)
    row_ts = lax.broadcasted_iota(I32, (TS, 1), 0)

    qit = qit_ref[...]
    w = wit_ref[...] * (IDX_HEADS ** -0.5)
    qi_pairs = [jnp.concatenate([qit[2 * p * IDX_DIM:(2 * p + 1) * IDX_DIM, :],
                                 qit[(2 * p + 1) * IDX_DIM:(2 * p + 2) * IDX_DIM, :]], axis=1)
                for p in range(IDX_HEADS // 2)]

    def score_tile(j, carry):
        kt = ki_ref[pl.ds(j * n_kb, n_kb)].reshape(TS, IDX_DIM)
        acc = jnp.zeros((TS, TQ), F32)
        for p in range(IDX_HEADS // 2):
            dd = jnp.dot(kt, qi_pairs[p], preferred_element_type=F32)
            acc = acc + w[2 * p:2 * p + 1, :] * jnp.maximum(dd[:, :TQ], 0.0)
            acc = acc + w[2 * p + 1:2 * p + 2, :] * jnp.maximum(dd[:, TQ:], 0.0)
        s_idx = j * TS + row_ts
        sc = jnp.where(s_idx <= t_idx, acc, -jnp.inf)
        sc = sc + 0.0
        bits = pltpu.bitcast(sc, I32)
        keys = bits ^ ((bits >> 31) & 0x7FFFFFFF)
        key_ref[pl.ds(j * n_kb, n_kb)] = keys.reshape(n_kb, KB, TQ)
        sc3 = sc.reshape(TS // 8, 8, TQ)
        for g in range(CAND_G):
            lst = [cand_ref[g, k] for k in range(CAND_M)]
            batch = _sorted_desc([sc3[g + CAND_G * c] for c in range(TS // 8 // CAND_G)])
            lst = _merge_top(lst, batch)
            for k in range(CAND_M):
                cand_ref[g, k] = lst[k]
        return carry

    cand_ref[...] = jnp.full(cand_ref.shape, -jnp.inf, F32)
    lax.fori_loop(0, nt, score_tile, 0)

    def to_keys(sc):
        bits = pltpu.bitcast(sc, I32)
        return bits ^ ((bits >> 31) & 0x7FFFFFFF)

    def count_ge(thr_signed):
        def body(j, cnt):
            keys = key_ref[pl.ds(j * n_kb, n_kb)].reshape(TS // 8, 8, TQ)
            return cnt + _tree_sum([jnp.where(keys[k] >= thr_signed, 1, 0) for k in range(TS // 8)])
        cnt = lax.fori_loop(0, nt, body, jnp.zeros((8, TQ), I32))
        return jnp.sum(cnt, axis=0, keepdims=True)

    def bisect(count_fn, prefix=None, nbits=32):
        def bit_step(bi, tb):
            cand = tb | jnp.left_shift(jnp.int32(1), nbits - 1 - bi)
            return jnp.where(count_fn(cand ^ INT_MIN) >= topk, cand, tb)
        tb = lax.fori_loop(0, nbits, bit_step, jnp.zeros((1, TQ), I32) if prefix is None else prefix)
        return jnp.maximum(tb ^ INT_MIN, NEG_INF_KEY + 1)

    stat_ref[2] = jnp.ones((8, TQ), I32)

    @pl.when(nt > CAND_MIN_TILES)
    def _():
        candk_ref[...] = to_keys(cand_ref[...].reshape(CAND_G * CAND_M, 8, TQ))

        def count_cand_ge(thr_signed):
            hits = [jnp.where(candk_ref[k] >= thr_signed, 1, 0) for k in range(CAND_G * CAND_M)]
            return jnp.sum(_tree_sum(hits), axis=0, keepdims=True)

        n_cls = 8 * CAND_G
        rank = -(-topk // n_cls)
        if rank <= CAND_M:
            rows_r = [candk_ref[g * CAND_M + rank - 1] for g in range(CAND_G)]
            hi_b = jnp.max(functools.reduce(jnp.maximum, rows_r), axis=0, keepdims=True) ^ INT_MIN
            lo_b = jnp.min(functools.reduce(jnp.minimum, rows_r), axis=0, keepdims=True) ^ INT_MIN
            diff = hi_b ^ lo_b
            expo = (pltpu.bitcast(diff.astype(F32), I32) >> 23) & 0xFF
            bits = jnp.where(diff < 0, 32, jnp.where(diff == 0, 0, expo - 126))
            nbits = jnp.minimum(jnp.max(bits), 32)
            low = jnp.where(nbits >= 32, jnp.int32(-1),
                            jnp.left_shift(jnp.int32(1), jnp.minimum(nbits, 31)) - 1)
            thr_c = bisect(count_cand_ge, hi_b & ~low, nbits)
        else:
            thr_c = bisect(count_cand_ge)
        stat_ref[0] = jnp.broadcast_to(thr_c, (8, TQ))
        stat_ref[1] = jnp.broadcast_to(count_ge(thr_c), (8, TQ))
        stat_ref[2] = jnp.broadcast_to((count_ge(thr_c + 1) >= topk).astype(I32), (8, TQ))

    @pl.when(jnp.max(stat_ref[2]) > 0)
    def _():
        thr_f = bisect(count_ge)
        stat_ref[0] = jnp.broadcast_to(thr_f, (8, TQ))
        stat_ref[1] = jnp.broadcast_to(count_ge(thr_f), (8, TQ))

    thr = stat_ref[0, 0:1, :]
    cnt_ge_thr = stat_ref[1, 0:1, :]
    excess = (cnt_ge_thr > topk)

    @pl.when(jnp.max(excess.astype(I32)) > 0)
    def _():
        need = topk - count_ge(thr + 1)
        sub = lax.broadcasted_iota(I32, (8, 1), 0)

        def demote(j, seen):
            keys = key_ref[pl.ds(j * n_kb, n_kb)].reshape(TS // 8, 8, TQ)
            out = []
            for v in range(TS // 8):
                kv = keys[v]
                eq = kv == thr
                p = jnp.where(eq, 1, 0)
                for sh in (1, 2, 4):
                    p = p + jnp.where(sub >= sh, pltpu.roll(p, sh, 0), 0)
                drop = excess & eq & (seen + p > need)
                out.append(jnp.where(drop, kv - 1, kv))
                seen = seen + p[7:8, :]
            key_ref[pl.ds(j * n_kb, n_kb)] = jnp.stack(out, axis=0).reshape(n_kb, KB, TQ)
            return seen
        lax.fori_loop(0, nt, demote, jnp.zeros((1, TQ), I32))

    qbt = qbt_ref[...]
    zq = jnp.zeros((HEAD_DIM_B, TQ), BF16)
    qb_pairs = [jnp.concatenate(
        [jnp.concatenate([qbt[2 * p * HEAD_DIM_B:(2 * p + 1) * HEAD_DIM_B, :], zq], axis=1),
         jnp.concatenate([zq, qbt[(2 * p + 1) * HEAD_DIM_B:(2 * p + 2) * HEAD_DIM_B, :]], axis=1)],
        axis=0) for p in range(N_HEADS_B // 2)]

    m_ref[...] = jnp.full(m_ref.shape, MASK_NEG, F32)
    l_ref[...] = jnp.zeros(l_ref.shape, F32)
    acc_ref[...] = jnp.zeros(acc_ref.shape, F32)

    def stage1(kt, madd, bias_of_head):
        n = kt.shape[0]
        for p in range(N_HEADS_B // 2):
            lg = jnp.dot(kt[:, 2 * p * HEAD_DIM_B:(2 * p + 2) * HEAD_DIM_B], qb_pairs[p],
                         preferred_element_type=F32)
            for hh in range(2):
                h = 2 * p + hh
                lo = lg[:, hh * TQ:(hh + 1) * TQ] + madd
                if bias_of_head is not None:
                    lo = lo + bias_of_head(h)
                lg_ref[h, 0:n, :] = lo
                mt_ref[h] = jnp.broadcast_to(jnp.max(lo, axis=0, keepdims=True), (8, TQ))

    def stage2(vt):
        n = vt.shape[1]
        for h in range(N_HEADS_B):
            m_old = m_ref[h]
            m_new = jnp.maximum(m_old, mt_ref[h])
            alpha = jnp.exp2(m_old - m_new)
            m_ref[h] = m_new
            pe = jnp.exp2(lg_ref[h, 0:n, :] - m_new[0:1, :]).astype(BF16)
            rs = slice(h * HEAD_DIM_B, (h + 1) * HEAD_DIM_B)
            vt_aug = jnp.concatenate([vt[rs, :], jnp.ones((16, n), BF16)], axis=0)
            pv = jnp.dot(vt_aug, pe, preferred_element_type=F32)
            l_ref[h] = alpha * l_ref[h] + pv[HEAD_DIM_B:HEAD_DIM_B + 8, :]
            acc_ref[rs, :] = alpha[0:1, :] * acc_ref[rs, :] + pv[0:HEAD_DIM_B, :]

    far_end = (i - 1) * KB
    nf = (i + n_fb - 2) // n_fb
    row_tf = lax.broadcasted_iota(I32, (TF, 1), 0)

    def far_tile(j, carry):
        kt = kb_ref[pl.ds(j * n_fb, n_fb)].reshape(TF, D_B)
        keys = key_ref[pl.ds(j * n_fb, n_fb)].reshape(TF, TQ)
        s_idx = j * TF + row_tf
        madd = jnp.where((keys >= thr) & (s_idx < far_end), 0.0, MASK_NEG)
        stage1(kt, madd, None)
        stage2(jnp.concatenate([vt_ref[j * n_fb + c] for c in range(n_fb)], axis=1))
        return carry
    lax.fori_loop(0, nf, far_tile, 0)

    nb0 = jnp.maximum(i - 1, 0)
    toff = jnp.where(i == 0, KB, 0)
    kt = kb_ref[pl.ds(nb0, NEAR // KB)].reshape(NEAR, D_B)
    keys = key_ref[pl.ds(nb0, NEAR // KB)].reshape(NEAR, TQ)
    madd = jnp.where(keys >= thr, 0.0, MASK_NEG)
    stage1(kt, madd, lambda h: bt_ref[h, pl.ds(pl.multiple_of(toff, KB), NEAR), :])
    stage2(jnp.concatenate([vt_ref[nb0 + c] for c in range(NEAR // KB)], axis=1))

    ot = jnp.concatenate([acc_ref[h * HEAD_DIM_B:(h + 1) * HEAD_DIM_B, :] * (1.0 / l_ref[h, 0:1, :])
                          for h in range(N_HEADS_B)], axis=0)
    ob = ot.T
    z = zb_ref[...].astype(F32)
    o_ref[...] = (ob * (z * jax.nn.sigmoid(z))).astype(o_ref.dtype)


def _dsa(qbt, qit, wit, g_in, kb3, vt3, ki3, btab, topk):
    L = qbt.shape[1]
    n_blk = L // KB
    whole = pl.BlockSpec(memory_space=pltpu.VMEM)
    return pl.pallas_call(
        functools.partial(_dsa_kernel, topk=topk),
        grid=(L // TQ,),
        in_specs=[pl.BlockSpec((D_B, TQ), lambda i: (0, i)),
                  pl.BlockSpec((IDX_HEADS * IDX_DIM, TQ), lambda i: (0, i)),
                  pl.BlockSpec((16, TQ), lambda i: (0, i)),
                  pl.BlockSpec((TQ, D_B), lambda i: (i, 1)),
                  whole, whole, whole, whole],
        out_specs=pl.BlockSpec((TQ, D_B), lambda i: (i, 0)),
        out_shape=jax.ShapeDtypeStruct((L, D_B), BF16),
        scratch_shapes=[pltpu.VMEM((n_blk, KB, TQ), I32),
                        pltpu.VMEM((CAND_G, CAND_M, 8, TQ), F32),
                        pltpu.VMEM((CAND_G * CAND_M, 8, TQ), I32),
                        pltpu.VMEM((3, 8, TQ), I32),
                        pltpu.VMEM((N_HEADS_B, TF, TQ), F32),
                        pltpu.VMEM((N_HEADS_B, 8, TQ), F32),
                        pltpu.VMEM((D_B, TQ), F32),
                        pltpu.VMEM((N_HEADS_B, 8, TQ), F32),
                        pltpu.VMEM((N_HEADS_B, 8, TQ), F32)],
        compiler_params=_cparams(("arbitrary",)),
        name="dsa",
    )(qbt, qit, wit, g_in, kb3, vt3, ki3, btab)


def _out_kernel(x_ref, ga_ref, gb_ref, a_ref, bq_ref, wa_ref, wb_ref, wo_ref, lg_ref, lb_ref,
                o_ref, o16_ref, *, alpha):
    ua = jnp.dot(a_ref[...], wa_ref[...], preferred_element_type=F32)
    ub = jnp.dot(bq_ref[...], wb_ref[...], preferred_element_type=F32)
    merged = (jax.nn.sigmoid(ga_ref[...].astype(F32)) * ua
              + jax.nn.sigmoid(gb_ref[...].astype(F32)) * ub)
    y = jnp.dot(merged.astype(BF16), wo_ref[...], preferred_element_type=F32)
    r = alpha * x_ref[...] + y
    mu = jnp.mean(r, axis=-1, keepdims=True)
    var = jnp.mean(jnp.square(r - mu), axis=-1, keepdims=True)
    out = (r - mu) * lax.rsqrt(var + LN_EPS) * lg_ref[...] + lb_ref[...]
    o_ref[...] = out
    o16_ref[...] = out.astype(BF16)


def _out_block(x, g_in, ga_in, gb_in, wa16, wb16, wo16, ln_g, ln_b, alpha, tm):
    L = x.shape[0]
    const = lambda shape: pl.BlockSpec(shape, lambda i: (0, 0))
    return pl.pallas_call(
        functools.partial(_out_kernel, alpha=alpha),
        grid=(L // tm,),
        in_specs=[pl.BlockSpec((tm, D_MODEL), lambda i: (i, 0)),
                  pl.BlockSpec((tm, D_MODEL), lambda i: (i, 1)),
                  pl.BlockSpec((tm, D_MODEL), lambda i: (i, 2)),
                  pl.BlockSpec((tm, D_A), lambda i: (i, 0)),
                  pl.BlockSpec((tm, D_B), lambda i: (i, 0)),
                  const((D_A, D_MODEL)), const((D_B, D_MODEL)), const((D_MODEL, D_MODEL)),
                  const((1, D_MODEL)), const((1, D_MODEL))],
        out_specs=[pl.BlockSpec((tm, D_MODEL), lambda i: (i, 0)),
                   pl.BlockSpec((tm, D_MODEL), lambda i: (i, 0))],
        out_shape=[jax.ShapeDtypeStruct((L, D_MODEL), F32),
                   jax.ShapeDtypeStruct((L, D_MODEL), BF16)],
        compiler_params=_cparams(("arbitrary",)),
        name="out_block",
    )(x, g_in, g_in, ga_in, gb_in, wa16, wb16, wo16, ln_g, ln_b)


def _prep_in_proj(w_in, b_in):
    c0 = 4 * D_A
    ci = c0 + 4 * D_B
    cw = ci + IDX_HEADS * IDX_DIM + IDX_DIM
    cg = cw + IDX_HEADS
    row_cols = [(0, 4 * D_A), (c0 + 3 * D_B, D_B), (cg, 2 * D_MODEL), (c0 + D_B, D_B),
                (ci + IDX_HEADS * IDX_DIM, IDX_DIM)]
    wn = jnp.concatenate([w_in[:, :, a:a + n] for a, n in row_cols], axis=2).astype(BF16)
    bn = jnp.concatenate([b_in[:, a:a + n] for a, n in row_cols], axis=1)[:, None, :]
    t_cols = [(c0, D_B, HEAD_DIM_B ** -0.5 * LOG2E), (ci, IDX_HEADS * IDX_DIM, IDX_DIM ** -0.5),
              (c0 + 2 * D_B, D_B, 1.0), (cw, IDX_HEADS, 1.0)]
    pad = PT_W - IDX_HEADS
    wt = jnp.concatenate([w_in[:, :, a:a + n] * s for a, n, s in t_cols], axis=2)
    wt = jnp.pad(wt, ((0, 0), (0, 0), (0, pad))).transpose(0, 2, 1).astype(BF16)
    bt = jnp.concatenate([b_in[:, a:a + n] * s for a, n, s in t_cols], axis=1)
    bt = jnp.pad(bt, ((0, 0), (0, pad)))[:, :, None]
    return wn, bn, wt, bt


def _layer(h, h16, wn, bn, wt, bt, wa16, wb16, wo16, lb, norm_g, btab, ln_g, ln_b, tril16, alpha, topk):
    L = h.shape[0]
    tm = min(512, L)
    a_in, g_in, kb, ki, qbt, qit, vt3, wit = _proj(h16, wn, bn, wt, bt, tm)
    ga_in = _hgrn(a_in, g_in, lb[None, :], norm_g[None, :], tril16)
    gb_in = _dsa(qbt, qit, wit, g_in, kb.reshape(L // KB, KB, D_B), vt3,
                 ki.reshape(L // KB, KB, IDX_DIM), btab, topk)
    return _out_block(h, g_in, ga_in, gb_in, wa16, wb16, wo16, ln_g[None, :], ln_b[None, :], alpha, tm)


def kernel(x, w_in, b_in, w_up_a, w_up_b, w_out, lb_logits, norm_a_g, rel_bias, ln_g, ln_b):
    depth = w_in.shape[0]
    batch, L, _ = x.shape
    alpha = (2 * depth) ** 0.25
    topk = min(TOPK_MAX, L // 4)
    lbs = jnp.cumsum(jax.nn.softmax(lb_logits.astype(F32), axis=0), axis=0)
    lbs = lbs - lbs[0:1]
    r = jnp.arange(HG_ROWS)
    tril16 = ((r[:, None] >= r[None, :]) & ((r[:, None] // HG_CHUNK) == (r[None, :] // HG_CHUNK))).astype(BF16)
    btab = _bias_table(rel_bias.astype(F32))
    wn, bn, wt, bt = _prep_in_proj(w_in, b_in)
    wa16, wb16, wo16 = w_up_a.astype(BF16), w_up_b.astype(BF16), w_out.astype(BF16)
    outs = []
    for bi in range(batch):
        h = x[bi]
        h16 = h.astype(BF16)
        for layer in range(depth):
            h, h16 = _layer(h, h16, wn[layer], bn[layer], wt[layer], bt[layer],
                            wa16[layer], wb16[layer], wo16[layer],
                            lbs[layer], norm_a_g[layer], btab, ln_g[layer], ln_b[layer], tril16, alpha, topk)
        outs.append(h)
    return jnp.stack(outs, axis=0)
```

```python
import functools
import math

import jax
import jax.numpy as jnp
from jax import lax
from jax.experimental import pallas as pl
from jax.experimental.pallas import tpu as pltpu

F32 = jnp.float32
BF16 = jnp.bfloat16
I32 = jnp.int32

D_MODEL = 1024
D_A = 512
HEAD_DIM_A = 128
N_HEADS_A = D_A // HEAD_DIM_A
N_HEADS_B = 8
HEAD_DIM_B = 64
D_B = N_HEADS_B * HEAD_DIM_B
IDX_HEADS = 8
IDX_DIM = 64
TOPK_MAX = 256
N_BUCKETS = 32
MAX_DISTANCE = 128
LN_EPS = 1e-5
RMS_EPS = 1e-6

LANES = 128
VMEM_LIMIT_BYTES = 58 * 1024 * 1024

HG_ROWS = 512
HG_CHUNK = 64
HG_SUB = 8

TQ = 128
KB = 128
TS = 1024
TF = 1024
NEAR = 256
MASK_NEG = -1e30
CAND_G = 16
CAND_M = 12
CAND_MIN_TILES = 2
LOG2E = 1.4426950408889634

INT_MIN = -2147483648
NEG_INF_KEY = -2139095041


def _cparams(sem):
    return pltpu.CompilerParams(dimension_semantics=sem, vmem_limit_bytes=VMEM_LIMIT_BYTES)


def _tree_sum(xs):
    xs = list(xs)
    while len(xs) > 1:
        nxt = [xs[a] + xs[a + 1] for a in range(0, len(xs) - 1, 2)]
        if len(xs) % 2:
            nxt.append(xs[-1])
        xs = nxt
    return xs[0]


def _oddeven_merge(lo, hi, r):
    step = r * 2
    if step < hi - lo:
        yield from _oddeven_merge(lo, hi, step)
        yield from _oddeven_merge(lo + r, hi, step)
        yield from [(a, a + r) for a in range(lo + r, hi - r, step)]
    else:
        yield (lo, lo + r)


def _oddeven_sort(lo, hi):
    if hi - lo >= 1:
        mid = lo + (hi - lo) // 2
        yield from _oddeven_sort(lo, mid)
        yield from _oddeven_sort(mid + 1, hi)
        yield from _oddeven_merge(lo, hi, 1)


def _sorted_desc(xs):
    xs = list(xs)
    for a, b in _oddeven_sort(0, len(xs) - 1):
        xs[a], xs[b] = jnp.maximum(xs[a], xs[b]), jnp.minimum(xs[a], xs[b])
    return xs


@functools.lru_cache(maxsize=None)
def _merge_top_plan(m, nb):
    n = 1
    while n < 2 * max(m, nb):
        n *= 2
    pos = [None] * n
    for k in range(m):
        pos[k] = ("l", k)
    for k in range(nb):
        pos[n // 2 + k] = ("b", k)
    ops = []
    for a, b in _oddeven_merge(0, n - 1, 1):
        va, vb = pos[a], pos[b]
        if vb is None:
            continue
        if va is None:
            pos[a], pos[b] = vb, None
            continue
        mx, mn = ("t", len(ops), 0), ("t", len(ops), 1)
        ops.append((mx, mn, va, vb))
        pos[a], pos[b] = mx, mn
    outs = tuple(pos[:m])
    live = set(outs)
    kept = []
    for mx, mn, va, vb in reversed(ops):
        if mx in live or mn in live:
            kept.append((mx if mx in live else None, mn if mn in live else None, va, vb))
            live.update((va, vb))
    return tuple(reversed(kept)), outs


def _merge_top(lst, batch):
    ops, outs = _merge_top_plan(len(lst), len(batch))
    env = {("l", k): x for k, x in enumerate(lst)}
    env.update({("b", k): x for k, x in enumerate(batch)})
    for mx, mn, va, vb in ops:
        if mx is not None:
            env[mx] = jnp.maximum(env[va], env[vb])
        if mn is not None:
            env[mn] = jnp.minimum(env[va], env[vb])
    return [env[o] for o in outs]


PN_A = 3 * D_A
PN_G = D_A + D_B + 2 * D_MODEL
PN = PN_A + PN_G + D_B + IDX_DIM
PT_W = 16
PT = D_B + IDX_HEADS * IDX_DIM + D_B + PT_W
PROJ_CHUNK = 512


def _proj_kernel(x_ref, wn_ref, bn_ref, wt_ref, bt_ref,
                 a_ref, g_ref, kb_ref, ki_ref, qbt_ref, qit_ref, vt_ref, wit_ref):
    x = x_ref[...]
    tm = x.shape[0]

    def rows(c0, n):
        return jnp.dot(x, wn_ref[:, c0:c0 + n], preferred_element_type=F32) + bn_ref[:, c0:c0 + n]

    def cols(r0, n):
        return lax.dot_general(wt_ref[r0:r0 + n, :], x, (((1,), (1,)), ((), ())),
                               preferred_element_type=F32) + bt_ref[r0:r0 + n, :]

    for c in range(PN_A // PROJ_CHUNK):
        a_ref[:, c * PROJ_CHUNK:(c + 1) * PROJ_CHUNK] = rows(c * PROJ_CHUNK, PROJ_CHUNK)
    for c in range(PN_G // PROJ_CHUNK):
        g_ref[:, c * PROJ_CHUNK:(c + 1) * PROJ_CHUNK] = rows(PN_A + c * PROJ_CHUNK, PROJ_CHUNK).astype(BF16)
    kb_ref[...] = rows(PN_A + PN_G, D_B).astype(BF16)
    ki_ref[...] = rows(PN_A + PN_G + D_B, IDX_DIM).astype(BF16)
    qbt_ref[...] = cols(0, D_B).astype(BF16)
    qit_ref[...] = cols(D_B, IDX_HEADS * IDX_DIM).astype(BF16)
    vt = cols(D_B + IDX_HEADS * IDX_DIM, D_B).astype(BF16)
    for c in range(tm // KB):
        vt_ref[c] = vt[:, c * KB:(c + 1) * KB]
    wit_ref[...] = cols(2 * D_B + IDX_HEADS * IDX_DIM, PT_W)


def _proj(x16, wn16, bn, wt16, bt, tm):
    L, k = x16.shape
    row = lambda n: pl.BlockSpec((tm, n), lambda i: (i, 0))
    colT = lambda n: pl.BlockSpec((n, tm), lambda i: (0, i))
    const = lambda shape: pl.BlockSpec(shape, lambda i: (0, 0))
    sds = jax.ShapeDtypeStruct
    return pl.pallas_call(
        _proj_kernel,
        grid=(L // tm,),
        in_specs=[row(k), const((k, PN)), const((1, PN)), const((PT, k)), const((PT, 1))],
        out_specs=[row(PN_A), row(PN_G), row(D_B), row(IDX_DIM),
                   colT(D_B), colT(IDX_HEADS * IDX_DIM),
                   pl.BlockSpec((tm // KB, D_B, KB), lambda i: (i, 0, 0)), colT(PT_W)],
        out_shape=[sds((L, PN_A), F32), sds((L, PN_G), BF16), sds((L, D_B), BF16), sds((L, IDX_DIM), BF16),
                   sds((D_B, L), BF16), sds((IDX_HEADS * IDX_DIM, L), BF16),
                   sds((L // KB, D_B, KB), BF16), sds((PT_W, L), F32)],
        compiler_params=_cparams(("arbitrary",)),
        name="proj",
    )(x16, wn16, bn, wt16, bt)


def _split3_dot(t16, x):
    x1 = x.astype(BF16)
    r1 = x - x1.astype(F32)
    x2 = r1.astype(BF16)
    r2 = r1 - x2.astype(F32)
    x3 = r2.astype(BF16)
    return (jnp.dot(t16, x1, preferred_element_type=F32)
            + jnp.dot(t16, x2, preferred_element_type=F32)
            + jnp.dot(t16, x3, preferred_element_type=F32))


def _hgrn_kernel(lb_ref, g_ref, q_ref, f_ref, i_ref, z_ref, tril_ref, o_ref, st_ref):
    @pl.when(pl.program_id(1) == 0)
    def _():
        st_ref[...] = jnp.zeros_like(st_ref)

    rows = q_ref.shape[0]
    lb = lb_ref[...]
    log_lb = jnp.log(lb)
    log_1m = jnp.log(1.0 - lb)
    fl = f_ref[...]
    log_sig = jnp.minimum(fl, 0.0) - jnp.log(1.0 + jnp.exp(-jnp.abs(fl)))
    bb = log_1m + log_sig
    mx = jnp.maximum(log_lb, bb)
    log_f = mx + jnp.log(1.0 + jnp.exp(-jnp.abs(log_lb - bb)))
    kk = (1.0 - lb) * jax.nn.sigmoid(-fl)
    b = _split3_dot(tril_ref[...], log_f * LOG2E)

    ones16 = jnp.ones((LANES, LANES), BF16)
    row_l = lax.broadcasted_iota(I32, (HG_CHUNK, 1), 0) % HG_SUB

    def diag_blocks(qc, kc, vc, bc):
        es = [(qc * kc).astype(BF16)]
        for d in range(1, HG_SUB):
            valid = (row_l + d) < HG_SUB
            qd = pltpu.roll(qc, HG_CHUNK - d, 0)
            bd = pltpu.roll(bc, HG_CHUNK - d, 0)
            dec = jnp.exp2(jnp.where(valid, bd - bc, 0.0))
            es.append(jnp.where(valid, qd * kc * dec, 0.0).astype(BF16))
        rs = jnp.dot(jnp.concatenate(es, axis=0), ones16, preferred_element_type=F32)
        o_d = rs[0:HG_CHUNK] * vc
        for d in range(1, HG_SUB):
            o_d = o_d + pltpu.roll(rs[d * HG_CHUNK:(d + 1) * HG_CHUNK] * vc, d, 0)
        return o_d

    tl = lax.broadcasted_iota(I32, (HG_CHUNK, HG_CHUNK), 0)
    sl = lax.broadcasted_iota(I32, (HG_CHUNK, HG_CHUNK), 1)
    rl = lax.broadcasted_iota(I32, (HG_CHUNK, 1), 0)
    widths = []
    w = HG_SUB
    while w < HG_CHUNK:
        widths.append(w)
        w *= 2
    pair_mask = {w: ((tl // (2 * w)) == (sl // (2 * w))) & ((tl % (2 * w)) >= w) & ((sl % (2 * w)) < w)
                 for w in widths}
    is_query = {w: (rl % (2 * w)) >= w for w in widths}

    def pair_scores(w, qc, kc, bc):
        hi = is_query[w]
        ref = bc[w - 1:w, :]
        for g in range(1, HG_CHUNK // (2 * w)):
            ref = jnp.where((rl // (2 * w)) == g, bc[g * 2 * w + w - 1:g * 2 * w + w, :], ref)
        ql = jnp.where(hi, qc * jnp.exp2(jnp.where(hi, bc - ref, 0.0)), 0.0)
        kl = jnp.where(hi, 0.0, kc * jnp.exp2(jnp.where(hi, 0.0, ref - bc)))
        s = lax.dot_general(ql.astype(BF16), kl.astype(BF16), (((1,), (1,)), ((), ())),
                            preferred_element_type=F32)
        return jnp.where(pair_mask[w], s, 0.0)

    st = st_ref[...]
    for c in range(rows // HG_CHUNK):
        sl_c = slice(c * HG_CHUNK, (c + 1) * HG_CHUNK)
        qc, vc = q_ref[sl_c, :], i_ref[sl_c, :]
        kc, bc = kk[sl_c], b[sl_c]
        b_last = bc[HG_CHUNK - 1:HG_CHUNK, :]
        p = pair_scores(widths[0], qc, kc, bc)
        for w in widths[1:]:
            p = p + pair_scores(w, qc, kc, bc)
        o_c = jnp.dot(p.astype(BF16), vc.astype(BF16), preferred_element_type=F32)
        o_c = o_c + diag_blocks(qc, kc, vc, bc)
        qe = (qc * jnp.exp2(bc)).astype(BF16)
        o_c = o_c + lax.dot_general(qe, st.astype(BF16), (((1,), (1,)), ((), ())),
                                    preferred_element_type=F32)
        ke = (kc * jnp.exp2(b_last - bc)).astype(BF16)
        upd = lax.dot_general(vc.astype(BF16), ke, (((0,), (0,)), ((), ())),
                              preferred_element_type=F32)
        st = st * jnp.exp2(b_last) + upd
        ms = jnp.mean(o_c * o_c, axis=-1, keepdims=True)
        oa = o_c * lax.rsqrt(ms + RMS_EPS) * g_ref[...]
        z = z_ref[sl_c, :].astype(F32)
        o_ref[sl_c, :] = (oa * (z * jax.nn.sigmoid(z))).astype(o_ref.dtype)
    st_ref[...] = st


def _hgrn(a_in, g_in, lb, norm_g, tril16):
    L = a_in.shape[0]
    nh = N_HEADS_A
    blk = lambda off: pl.BlockSpec((HG_ROWS, HEAD_DIM_A), lambda h, s: (s, off + h))
    return pl.pallas_call(
        _hgrn_kernel,
        grid=(nh, L // HG_ROWS),
        in_specs=[pl.BlockSpec((1, HEAD_DIM_A), lambda h, s: (0, h)),
                  pl.BlockSpec((1, HEAD_DIM_A), lambda h, s: (0, h)),
                  blk(0), blk(nh), blk(2 * nh),
                  pl.BlockSpec((HG_ROWS, HEAD_DIM_A), lambda h, s: (s, h)),
                  pl.BlockSpec((HG_ROWS, HG_ROWS), lambda h, s: (0, 0))],
        out_specs=pl.BlockSpec((HG_ROWS, HEAD_DIM_A), lambda h, s: (s, h)),
        out_shape=jax.ShapeDtypeStruct((L, D_A), BF16),
        scratch_shapes=[pltpu.VMEM((HEAD_DIM_A, HEAD_DIM_A), F32)],
        compiler_params=_cparams(("arbitrary", "arbitrary")),
        name="hgrn2",
    )(lb, norm_g, a_in, a_in, a_in, g_in, tril16)


def _bias_table_kernel(rb_ref, o_ref):
    n_c = o_ref.shape[1]
    c = lax.broadcasted_iota(I32, (n_c, TQ), 0)
    t = lax.broadcasted_iota(I32, (n_c, TQ), 1)
    dist = t + KB - c
    max_exact = N_BUCKETS // 2
    d = jnp.maximum(dist, 0)
    df = jnp.maximum(d, 1).astype(F32)
    large = max_exact + (jnp.log(df / max_exact) / math.log(MAX_DISTANCE / max_exact)
                         * (N_BUCKETS - max_exact)).astype(I32)
    large = jnp.minimum(large, N_BUCKETS - 1)
    bucket = jnp.where(d < max_exact, d, large)
    for h in range(N_HEADS_B):
        acc = jnp.zeros((n_c, TQ), F32)
        for k in range(N_BUCKETS):
            acc = jnp.where(bucket == k, rb_ref[k, h], acc)
        o_ref[h] = (acc - rb_ref[N_BUCKETS - 1, h]) * LOG2E


def _bias_table(rel_bias):
    return pl.pallas_call(
        _bias_table_kernel,
        in_specs=[pl.BlockSpec(memory_space=pltpu.SMEM)],
        out_specs=pl.BlockSpec(memory_space=pltpu.VMEM),
        out_shape=jax.ShapeDtypeStruct((N_HEADS_B, NEAR + KB, TQ), F32),
        name="t5_bias_table",
    )(rel_bias)


def _dsa_kernel(qbt_ref, qit_ref, wit_ref, zb_ref, kb_ref, vt_ref, ki_ref, bt_ref, o_ref,
                key_ref, cand_ref, candk_ref, stat_ref, lg_ref, mt_ref, acc_ref, m_ref, l_ref,
                *, topk):
    i = pl.program_id(0)
    t_idx = i * TQ + lax.broadcasted_iota(I32, (1, TQ), 1)
    n_kb = TS // KB
    n_fb = TF // KB
    nt = (TF // TS) * ((i + n_fb) // n_fb)
    row_ts = lax.broadcasted_iota(I32, (TS, 1), 0)

    qit = qit_ref[...]
    w = wit_ref[...] * (IDX_HEADS ** -0.5)
    qi_pairs = [jnp.concatenate([qit[2 * p * IDX_DIM:(2 * p + 1) * IDX_DIM, :],
                                 qit[(2 * p + 1) * IDX_DIM:(2 * p + 2) * IDX_DIM, :]], axis=1)
                for p in range(IDX_HEADS // 2)]

    def score_tile(j, carry, has_future_keys):
        kt = ki_ref[pl.ds(j * n_kb, n_kb)].reshape(TS, IDX_DIM)
        acc = jnp.zeros((TS, TQ), F32)
        for p in range(IDX_HEADS // 2):
            dd = jnp.dot(kt, qi_pairs[p], preferred_element_type=F32)
            acc = acc + w[2 * p:2 * p + 1, :] * jnp.maximum(dd[:, :TQ], 0.0)
            acc = acc + w[2 * p + 1:2 * p + 2, :] * jnp.maximum(dd[:, TQ:], 0.0)
        sc = jnp.where(j * TS + row_ts <= t_idx, acc, -jnp.inf) if has_future_keys else acc
        sc = sc + 0.0
        bits = pltpu.bitcast(sc, I32)
        keys = bits ^ ((bits >> 31) & 0x7FFFFFFF)
        key_ref[pl.ds(j * n_kb, n_kb)] = keys.reshape(n_kb, KB, TQ)
        sc3 = sc.reshape(TS // 8, 8, TQ)
        for g in range(CAND_G):
            lst = [cand_ref[g, k] for k in range(CAND_M)]
            batch = _sorted_desc([sc3[g + CAND_G * c] for c in range(TS // 8 // CAND_G)])
            lst = _merge_top(lst, batch)
            for k in range(CAND_M):
                cand_ref[g, k] = lst[k]
        return carry

    cand_ref[...] = jnp.full(cand_ref.shape, -jnp.inf, F32)
    lax.fori_loop(0, nt - 1, functools.partial(score_tile, has_future_keys=False), 0)
    score_tile(nt - 1, 0, has_future_keys=True)

    def to_keys(sc):
        bits = pltpu.bitcast(sc, I32)
        return bits ^ ((bits >> 31) & 0x7FFFFFFF)

    def count_ge(thr_signed):
        def body(j, cnt):
            keys = key_ref[pl.ds(j * n_kb, n_kb)].reshape(TS // 8, 8, TQ)
            return cnt + _tree_sum([jnp.where(keys[k] >= thr_signed, 1, 0) for k in range(TS // 8)])
        cnt = lax.fori_loop(0, nt, body, jnp.zeros((8, TQ), I32))
        return jnp.sum(cnt, axis=0, keepdims=True)

    def bisect(count_fn, prefix=None, nbits=32):
        def bit_step(bi, tb):
            cand = tb | jnp.left_shift(jnp.int32(1), nbits - 1 - bi)
            return jnp.where(count_fn(cand ^ INT_MIN) >= topk, cand, tb)
        tb = lax.fori_loop(0, nbits, bit_step, jnp.zeros((1, TQ), I32) if prefix is None else prefix)
        return jnp.maximum(tb ^ INT_MIN, NEG_INF_KEY + 1)

    stat_ref[2] = jnp.ones((8, TQ), I32)

    @pl.when(nt > CAND_MIN_TILES)
    def _():
        candk_ref[...] = to_keys(cand_ref[...].reshape(CAND_G * CAND_M, 8, TQ))

        def count_cand_ge(thr_signed):
            hits = [jnp.where(candk_ref[k] >= thr_signed, 1, 0) for k in range(CAND_G * CAND_M)]
            return jnp.sum(_tree_sum(hits), axis=0, keepdims=True)

        n_cls = 8 * CAND_G
        rank = -(-topk // n_cls)
        if rank <= CAND_M:
            rows_r = [candk_ref[g * CAND_M + rank - 1] for g in range(CAND_G)]
            hi_b = jnp.max(functools.reduce(jnp.maximum, rows_r), axis=0, keepdims=True) ^ INT_MIN
            lo_b = jnp.min(functools.reduce(jnp.minimum, rows_r), axis=0, keepdims=True) ^ INT_MIN
            diff = hi_b ^ lo_b
            expo = (pltpu.bitcast(diff.astype(F32), I32) >> 23) & 0xFF
            bits = jnp.where(diff < 0, 32, jnp.where(diff == 0, 0, expo - 126))
            nbits = jnp.minimum(jnp.max(bits), 32)
            low = jnp.where(nbits >= 32, jnp.int32(-1),
                            jnp.left_shift(jnp.int32(1), jnp.minimum(nbits, 31)) - 1)
            thr_c = bisect(count_cand_ge, hi_b & ~low, nbits)
        else:
            thr_c = bisect(count_cand_ge)
        stat_ref[0] = jnp.broadcast_to(thr_c, (8, TQ))
        stat_ref[1] = jnp.broadcast_to(count_ge(thr_c), (8, TQ))
        stat_ref[2] = jnp.broadcast_to((count_ge(thr_c + 1) >= topk).astype(I32), (8, TQ))

    @pl.when(jnp.max(stat_ref[2]) > 0)
    def _():
        thr_f = bisect(count_ge)
        stat_ref[0] = jnp.broadcast_to(thr_f, (8, TQ))
        stat_ref[1] = jnp.broadcast_to(count_ge(thr_f), (8, TQ))

    thr = stat_ref[0, 0:1, :]
    cnt_ge_thr = stat_ref[1, 0:1, :]
    excess = (cnt_ge_thr > topk)

    @pl.when(jnp.max(excess.astype(I32)) > 0)
    def _():
        need = topk - count_ge(thr + 1)
        sub = lax.broadcasted_iota(I32, (8, 1), 0)

        def demote(j, seen):
            keys = key_ref[pl.ds(j * n_kb, n_kb)].reshape(TS // 8, 8, TQ)
            out = []
            for v in range(TS // 8):
                kv = keys[v]
                eq = kv == thr
                p = jnp.where(eq, 1, 0)
                for sh in (1, 2, 4):
                    p = p + jnp.where(sub >= sh, pltpu.roll(p, sh, 0), 0)
                drop = excess & eq & (seen + p > need)
                out.append(jnp.where(drop, kv - 1, kv))
                seen = seen + p[7:8, :]
            key_ref[pl.ds(j * n_kb, n_kb)] = jnp.stack(out, axis=0).reshape(n_kb, KB, TQ)
            return seen
        lax.fori_loop(0, nt, demote, jnp.zeros((1, TQ), I32))

    qbt = qbt_ref[...]
    zq = jnp.zeros((HEAD_DIM_B, TQ), BF16)
    qb_pairs = [jnp.concatenate(
        [jnp.concatenate([qbt[2 * p * HEAD_DIM_B:(2 * p + 1) * HEAD_DIM_B, :], zq], axis=1),
         jnp.concatenate([zq, qbt[(2 * p + 1) * HEAD_DIM_B:(2 * p + 2) * HEAD_DIM_B, :]], axis=1)],
        axis=0) for p in range(N_HEADS_B // 2)]

    eye_rows = lax.broadcasted_iota(I32, (TQ, 2 * TQ), 0)
    eye_cols = lax.broadcasted_iota(I32, (TQ, 2 * TQ), 1)
    eye2 = jnp.where((eye_cols == eye_rows) | (eye_cols == eye_rows + TQ), 1.0, 0.0).astype(BF16)
    qb_pairs_eye = [jnp.concatenate([qp, eye2], axis=0) for qp in qb_pairs]

    m_ref[...] = jnp.full(m_ref.shape, MASK_NEG, F32)
    l_ref[...] = jnp.zeros(l_ref.shape, F32)
    acc_ref[...] = jnp.zeros(acc_ref.shape, F32)

    def stage1(kt, madd, bias_of_head):
        n = kt.shape[0]
        fold_mask = bias_of_head is None
        if fold_mask:
            madd16 = madd.astype(BF16)
        for p in range(N_HEADS_B // 2):
            kp = kt[:, 2 * p * HEAD_DIM_B:(2 * p + 2) * HEAD_DIM_B]
            if fold_mask:
                lg = jnp.dot(jnp.concatenate([kp, madd16], axis=1), qb_pairs_eye[p],
                             preferred_element_type=F32)
            else:
                lg = jnp.dot(kp, qb_pairs[p], preferred_element_type=F32)
            for hh in range(2):
                h = 2 * p + hh
                lo = lg[:, hh * TQ:(hh + 1) * TQ]
                if not fold_mask:
                    lo = lo + madd + bias_of_head(h)
                lg_ref[h, 0:n, :] = lo
                mt_ref[h] = jnp.broadcast_to(jnp.max(lo, axis=0, keepdims=True), (8, TQ))

    def stage2(vt):
        n = vt.shape[1]
        for h in range(N_HEADS_B):
            m_old = m_ref[h]
            m_new = jnp.maximum(m_old, mt_ref[h])
            alpha = jnp.exp2(m_old - m_new)
            m_ref[h] = m_new
            pe = jnp.exp2(lg_ref[h, 0:n, :] - m_new[0:1, :]).astype(BF16)
            rs = slice(h * HEAD_DIM_B, (h + 1) * HEAD_DIM_B)
            vt_aug = jnp.concatenate([vt[rs, :], jnp.ones((16, n), BF16)], axis=0)
            pv = jnp.dot(vt_aug, pe, preferred_element_type=F32)
            l_ref[h] = alpha * l_ref[h] + pv[HEAD_DIM_B:HEAD_DIM_B + 8, :]
            acc_ref[rs, :] = alpha[0:1, :] * acc_ref[rs, :] + pv[0:HEAD_DIM_B, :]

    far_end = (i - 1) * KB
    nf = (i + n_fb - 2) // n_fb
    row_tf = lax.broadcasted_iota(I32, (TF, 1), 0)

    def far_tile(j, carry):
        kt = kb_ref[pl.ds(j * n_fb, n_fb)].reshape(TF, D_B)
        keys = key_ref[pl.ds(j * n_fb, n_fb)].reshape(TF, TQ)
        s_idx = j * TF + row_tf
        madd = jnp.where((keys >= thr) & (s_idx < far_end), 0.0, MASK_NEG)
        stage1(kt, madd, None)
        stage2(jnp.concatenate([vt_ref[j * n_fb + c] for c in range(n_fb)], axis=1))
        return carry
    lax.fori_loop(0, nf, far_tile, 0)

    nb0 = jnp.maximum(i - 1, 0)
    toff = jnp.where(i == 0, KB, 0)
    kt = kb_ref[pl.ds(nb0, NEAR // KB)].reshape(NEAR, D_B)
    keys = key_ref[pl.ds(nb0, NEAR // KB)].reshape(NEAR, TQ)
    madd = jnp.where(keys >= thr, 0.0, MASK_NEG)
    stage1(kt, madd, lambda h: bt_ref[h, pl.ds(pl.multiple_of(toff, KB), NEAR), :])
    stage2(jnp.concatenate([vt_ref[nb0 + c] for c in range(NEAR // KB)], axis=1))

    ot = jnp.concatenate([acc_ref[h * HEAD_DIM_B:(h + 1) * HEAD_DIM_B, :] * (1.0 / l_ref[h, 0:1, :])
                          for h in range(N_HEADS_B)], axis=0)
    ob = ot.T
    z = zb_ref[...].astype(F32)
    o_ref[...] = (ob * (z * jax.nn.sigmoid(z))).astype(o_ref.dtype)


def _dsa(qbt, qit, wit, g_in, kb3, vt3, ki3, btab, topk):
    L = qbt.shape[1]
    n_blk = L // KB
    whole = pl.BlockSpec(memory_space=pltpu.VMEM)
    return pl.pallas_call(
        functools.partial(_dsa_kernel, topk=topk),
        grid=(L // TQ,),
        in_specs=[pl.BlockSpec((D_B, TQ), lambda i: (0, i)),
                  pl.BlockSpec((IDX_HEADS * IDX_DIM, TQ), lambda i: (0, i)),
                  pl.BlockSpec((16, TQ), lambda i: (0, i)),
                  pl.BlockSpec((TQ, D_B), lambda i: (i, 1)),
                  whole, whole, whole, whole],
        out_specs=pl.BlockSpec((TQ, D_B), lambda i: (i, 0)),
        out_shape=jax.ShapeDtypeStruct((L, D_B), BF16),
        scratch_shapes=[pltpu.VMEM((n_blk, KB, TQ), I32),
                        pltpu.VMEM((CAND_G, CAND_M, 8, TQ), F32),
                        pltpu.VMEM((CAND_G * CAND_M, 8, TQ), I32),
                        pltpu.VMEM((3, 8, TQ), I32),
                        pltpu.VMEM((N_HEADS_B, TF, TQ), F32),
                        pltpu.VMEM((N_HEADS_B, 8, TQ), F32),
                        pltpu.VMEM((D_B, TQ), F32),
                        pltpu.VMEM((N_HEADS_B, 8, TQ), F32),
                        pltpu.VMEM((N_HEADS_B, 8, TQ), F32)],
        compiler_params=_cparams(("arbitrary",)),
        name="dsa",
    )(qbt, qit, wit, g_in, kb3, vt3, ki3, btab)


def _out_kernel(x_ref, ga_ref, gb_ref, a_ref, bq_ref, wa_ref, wb_ref, wo_ref, lg_ref, lb_ref,
                o_ref, o16_ref, *, alpha):
    ua = jnp.dot(a_ref[...], wa_ref[...], preferred_element_type=F32)
    ub = jnp.dot(bq_ref[...], wb_ref[...], preferred_element_type=F32)
    merged = (jax.nn.sigmoid(ga_ref[...].astype(F32)) * ua
              + jax.nn.sigmoid(gb_ref[...].astype(F32)) * ub)
    y = jnp.dot(merged.astype(BF16), wo_ref[...], preferred_element_type=F32)
    r = alpha * x_ref[...] + y
    mu = jnp.mean(r, axis=-1, keepdims=True)
    var = jnp.mean(jnp.square(r - mu), axis=-1, keepdims=True)
    out = (r - mu) * lax.rsqrt(var + LN_EPS) * lg_ref[...] + lb_ref[...]
    o_ref[...] = out
    o16_ref[...] = out.astype(BF16)


def _out_block(x, g_in, ga_in, gb_in, wa16, wb16, wo16, ln_g, ln_b, alpha, tm):
    L = x.shape[0]
    const = lambda shape: pl.BlockSpec(shape, lambda i: (0, 0))
    return pl.pallas_call(
        functools.partial(_out_kernel, alpha=alpha),
        grid=(L // tm,),
        in_specs=[pl.BlockSpec((tm, D_MODEL), lambda i: (i, 0)),
                  pl.BlockSpec((tm, D_MODEL), lambda i: (i, 1)),
                  pl.BlockSpec((tm, D_MODEL), lambda i: (i, 2)),
                  pl.BlockSpec((tm, D_A), lambda i: (i, 0)),
                  pl.BlockSpec((tm, D_B), lambda i: (i, 0)),
                  const((D_A, D_MODEL)), const((D_B, D_MODEL)), const((D_MODEL, D_MODEL)),
                  const((1, D_MODEL)), const((1, D_MODEL))],
        out_specs=[pl.BlockSpec((tm, D_MODEL), lambda i: (i, 0)),
                   pl.BlockSpec((tm, D_MODEL), lambda i: (i, 0))],
        out_shape=[jax.ShapeDtypeStruct((L, D_MODEL), F32),
                   jax.ShapeDtypeStruct((L, D_MODEL), BF16)],
        compiler_params=_cparams(("arbitrary",)),
        name="out_block",
    )(x, g_in, g_in, ga_in, gb_in, wa16, wb16, wo16, ln_g, ln_b)


def _prep_in_proj(w_in, b_in):
    c0 = 4 * D_A
    ci = c0 + 4 * D_B
    cw = ci + IDX_HEADS * IDX_DIM + IDX_DIM
    cg = cw + IDX_HEADS
    row_cols = [(0, 4 * D_A), (c0 + 3 * D_B, D_B), (cg, 2 * D_MODEL), (c0 + D_B, D_B),
                (ci + IDX_HEADS * IDX_DIM, IDX_DIM)]
    wn = jnp.concatenate([w_in[:, :, a:a + n] for a, n in row_cols], axis=2).astype(BF16)
    bn = jnp.concatenate([b_in[:, a:a + n] for a, n in row_cols], axis=1)[:, None, :]
    t_cols = [(c0, D_B, HEAD_DIM_B ** -0.5 * LOG2E), (ci, IDX_HEADS * IDX_DIM, IDX_DIM ** -0.5),
              (c0 + 2 * D_B, D_B, 1.0), (cw, IDX_HEADS, 1.0)]
    pad = PT_W - IDX_HEADS
    wt = jnp.concatenate([w_in[:, :, a:a + n] * s for a, n, s in t_cols], axis=2)
    wt = jnp.pad(wt, ((0, 0), (0, 0), (0, pad))).transpose(0, 2, 1).astype(BF16)
    bt = jnp.concatenate([b_in[:, a:a + n] * s for a, n, s in t_cols], axis=1)
    bt = jnp.pad(bt, ((0, 0), (0, pad)))[:, :, None]
    return wn, bn, wt, bt


def _layer(h, h16, wn, bn, wt, bt, wa16, wb16, wo16, lb, norm_g, btab, ln_g, ln_b, tril16, alpha, topk):
    L = h.shape[0]
    tm = min(512, L)
    a_in, g_in, kb, ki, qbt, qit, vt3, wit = _proj(h16, wn, bn, wt, bt, tm)
    ga_in = _hgrn(a_in, g_in, lb[None, :], norm_g[None, :], tril16)
    gb_in = _dsa(qbt, qit, wit, g_in, kb.reshape(L // KB, KB, D_B), vt3,
                 ki.reshape(L // KB, KB, IDX_DIM), btab, topk)
    return _out_block(h, g_in, ga_in, gb_in, wa16, wb16, wo16, ln_g[None, :], ln_b[None, :], alpha, tm)


def kernel(x, w_in, b_in, w_up_a, w_up_b, w_out, lb_logits, norm_a_g, rel_bias, ln_g, ln_b):
    depth = w_in.shape[0]
    batch, L, _ = x.shape
    alpha = (2 * depth) ** 0.25
    topk = min(TOPK_MAX, L // 4)
    lbs = jnp.cumsum(jax.nn.softmax(lb_logits.astype(F32), axis=0), axis=0)
    lbs = lbs - lbs[0:1]
    r = jnp.arange(HG_ROWS)
    tril16 = ((r[:, None] >= r[None, :]) & ((r[:, None] // HG_CHUNK) == (r[None, :] // HG_CHUNK))).astype(BF16)
    btab = _bias_table(rel_bias.astype(F32))
    wn, bn, wt, bt = _prep_in_proj(w_in, b_in)
    wa16, wb16, wo16 = w_up_a.astype(BF16), w_up_b.astype(BF16), w_out.astype(BF16)
    outs = []
    for bi in range(batch):
        h = x[bi]
        h16 = h.astype(BF16)
        for layer in range(depth):
            h, h16 = _layer(h, h16, wn[layer], bn[layer], wt[layer], bt[layer],
                            wa16[layer], wb16[layer], wo16[layer],
                            lbs[layer], norm_a_g[layer], btab, ln_g[layer], ln_b[layer], tril16, alpha, topk)
        outs.append(h)
    return jnp.stack(outs, axis=0)
```
